```python
import math
import jax, jax.numpy as jnp
from jax import lax
import numpy as np

D_MODEL = 1024
BATCH = 16
SEQ = 4096
DEPTH = 2

N_MEM = 256
D_MIX = D_MODEL
GROUP_WIDTH = D_MIX // 4
HEAD_DIM = 64
N_HEADS = GROUP_WIDTH // HEAD_DIM
RET_QK_DIM = HEAD_DIM // 2
RET_CHUNK = 128
RW_DECAY_LORA = 64
RW_A_LORA = 64
RW_GATE_LORA = 128
RW_DECAY_SCALE = math.exp(-0.5)
RW_GN_EPS = 64e-5
SSM_STATE = 128
SSM_GROUPS = 2
SSM_CONV = 4
SSM_CHUNK = 128
SSM_XBC = GROUP_WIDTH + 2 * SSM_GROUPS * SSM_STATE
DSA_KV_DIM = HEAD_DIM
IDX_HEADS = 4
IDX_DIM = 32
DSA_TOPK_MAX = 256
DSA_QBLOCK = 128
X_HEADS = 4
X_HEAD_DIM = D_MODEL // X_HEADS
D_FF = 2816
FFN_CONV = 3
ROPE_THETA = 10000.0
NORM_EPS = 1e-6

RET_SPLITS = (N_HEADS * RET_QK_DIM, N_HEADS * RET_QK_DIM, GROUP_WIDTH, GROUP_WIDTH)
RW_SPLITS = (GROUP_WIDTH, GROUP_WIDTH, GROUP_WIDTH, RW_DECAY_LORA, RW_A_LORA, RW_GATE_LORA)
SSM_SPLITS = (GROUP_WIDTH, SSM_XBC, N_HEADS)
DSA_SPLITS = (GROUP_WIDTH, DSA_KV_DIM, DSA_KV_DIM, IDX_HEADS * IDX_DIM, IDX_DIM, IDX_HEADS)
GROUP_COLS = (sum(RET_SPLITS), sum(RW_SPLITS), sum(SSM_SPLITS), sum(DSA_SPLITS))
N_IN = sum(GROUP_COLS)

kernel_name = "hybrid_headgroup_ret_rwkv7_ssd_dsa_block"


def split_cols(t, sizes):
    idx = [int(i) for i in np.cumsum(sizes)[:-1]]
    return jnp.split(t, idx, axis=-1)


def rms_norm(x, g, eps=NORM_EPS):
    xf = x.astype(jnp.float32)
    y = xf * lax.rsqrt(jnp.mean(xf * xf, axis=-1, keepdims=True) + eps)
    return (y * g.astype(jnp.float32)).astype(x.dtype)


def head_rms(y, eps=NORM_EPS):
    yf = y.astype(jnp.float32)
    return (yf * lax.rsqrt(jnp.mean(yf * yf, axis=-1, keepdims=True) + eps)).astype(y.dtype)


def head_layer_norm(y, eps):
    yf = y.astype(jnp.float32)
    mu = jnp.mean(yf, axis=-1, keepdims=True)
    var = jnp.mean(jnp.square(yf - mu), axis=-1, keepdims=True)
    return ((yf - mu) * lax.rsqrt(var + eps)).astype(y.dtype)


def rope(x, pos):
    half = x.shape[-1] // 2
    inv = ROPE_THETA ** (-jnp.arange(half, dtype=jnp.float32) / half)
    ang = pos.astype(jnp.float32)[:, None] * inv[None, :]
    cos = jnp.cos(ang)[:, None, :]
    sin = jnp.sin(ang)[:, None, :]
    xf = x.astype(jnp.float32)
    x1, x2 = xf[..., :half], xf[..., half:]
    return jnp.concatenate([x1 * cos - x2 * sin, x2 * cos + x1 * sin], axis=-1).astype(x.dtype)


def causal_dwconv(x, w, b):
    width, ch = w.shape
    y = lax.conv_general_dilated(x, w[:, None, :].astype(x.dtype), window_strides=(1,),
                                 padding=[(width - 1, 0)],
                                 dimension_numbers=('NWC', 'WIO', 'NWC'),
                                 feature_group_count=ch)
    return y + b.astype(x.dtype)


def token_shift_mix(f, mu):
    prev = jnp.pad(f, ((0, 0), (1, 0), (0, 0)))[:, :-1]
    return f + (prev - f) * mu


def chunked_decay_recurrence(q, k, v, log_a, chunk):
    out_dtype = v.dtype
    b, s, h, n = q.shape
    p = v.shape[-1]
    nc = s // chunk
    qc = q.astype(jnp.float32).reshape(b, nc, chunk, h, n)
    kc = k.astype(jnp.float32).reshape(b, nc, chunk, h, n)
    vc = v.astype(jnp.float32).reshape(b, nc, chunk, h, p)
    cum = jnp.cumsum(log_a.astype(jnp.float32).reshape(b, nc, chunk, h), axis=2)
    causal = jnp.tril(jnp.ones((chunk, chunk), dtype=bool))[None, None, :, :, None]
    seg = cum[:, :, :, None, :] - cum[:, :, None, :, :]
    decay = jnp.exp(jnp.where(causal, seg, -jnp.inf))
    scores = jnp.einsum('bctHn,bcsHn->bctsH', qc, kc) * decay
    y_intra = jnp.einsum('bctsH,bcsHp->bctHp', scores, vc)
    decay_to_end = jnp.exp(cum[:, :, -1:, :] - cum)
    chunk_state = jnp.einsum('bcsHn,bcsH,bcsHp->bcHnp', kc, decay_to_end, vc)
    chunk_decay = jnp.exp(cum[:, :, -1, :])

    def step(state, inp):
        cs, cd = inp
        return state * cd[:, :, None, None] + cs, state

    init = jnp.zeros((b, h, n, p), jnp.float32)
    _, prev_states = lax.scan(step, init, (jnp.moveaxis(chunk_state, 1, 0), jnp.moveaxis(chunk_decay, 1, 0)))
    prev_states = jnp.moveaxis(prev_states, 0, 1)
    y_inter = jnp.einsum('bctHn,bcHnp,bctH->bctHp', qc, prev_states, jnp.exp(cum))
    return (y_intra + y_inter).reshape(b, s, h, p).astype(out_dtype)


def retention_group(q, k, v, g, pos):
    b, s, _ = q.shape
    q = rope(q.reshape(b, s, N_HEADS, RET_QK_DIM), pos)
    k = rope(k.reshape(b, s, N_HEADS, RET_QK_DIM), pos) * (RET_QK_DIM ** -0.5)
    v = v.reshape(b, s, N_HEADS, HEAD_DIM)
    log_gamma = jnp.log(1.0 - jnp.power(2.0, -5.0 - jnp.arange(N_HEADS, dtype=jnp.float32)))
    la = jnp.broadcast_to(log_gamma, (b, s, N_HEADS))
    o = head_rms(chunked_decay_recurrence(q, k, v, la, RET_CHUNK))
    return o.reshape(b, s, GROUP_WIDTH) * jax.nn.silu(g)


def rwkv7_scan(r, w, k, v, kk, a):
    seq = [jnp.moveaxis(t.astype(jnp.float32), 1, 0) for t in (r, w, k, v, kk, a)]
    b, _, h, n = r.shape

    def step(state, inp):
        r_t, w_t, k_t, v_t, kk_t, a_t = inp
        sa = jnp.einsum('bhvk,bhk->bhv', state, -kk_t)
        state = (state * w_t[:, :, None, :] + sa[..., None] * (kk_t * a_t)[:, :, None, :]
                 + v_t[..., None] * k_t[:, :, None, :])
        return state, jnp.einsum('bhvk,bhk->bhv', state, r_t)

    init = jnp.zeros((b, h, n, n), jnp.float32)
    _, y = lax.scan(step, init, tuple(seq))
    return jnp.moveaxis(y, 0, 1).astype(r.dtype)


def rwkv7_group(r, k, v, wl, al, gl, w0, w2, a0, a2, g2, k_k, k_a, r_k, ln_w, ln_b):
    b, s, _ = r.shape
    hd = (b, s, N_HEADS, HEAD_DIM)
    log_w = -RW_DECAY_SCALE * jax.nn.sigmoid((w0 + jnp.tanh(wl) @ w2).astype(jnp.float32))
    w = jnp.exp(log_w)
    a = jax.nn.sigmoid(a0 + al @ a2)
    g = jax.nn.sigmoid(gl) @ g2
    kk = (k * k_k).reshape(hd)
    kk = kk * lax.rsqrt(jnp.sum(kk * kk, axis=-1, keepdims=True) + 1e-12)
    k = k * (1.0 + (a - 1.0) * k_a)
    rh, kh, vh = r.reshape(hd), k.reshape(hd), v.reshape(hd)
    y = rwkv7_scan(rh, w.reshape(hd), kh, vh, kk, a.reshape(hd))
    y = head_layer_norm(y, RW_GN_EPS).reshape(b, s, GROUP_WIDTH) * ln_w + ln_b
    bonus = (jnp.sum(rh * kh * r_k, axis=-1, keepdims=True) * vh).reshape(b, s, GROUP_WIDTH)
    return (y + bonus) * g


def ssd_group(z, xbc, dt, conv_w, conv_b, dt_bias, a_log, d_skip, norm_w):
    b, s, _ = z.shape
    xbc = jax.nn.silu(causal_dwconv(xbc, conv_w, conv_b))
    xs, bm, cm = split_cols(xbc, (GROUP_WIDTH, SSM_GROUPS * SSM_STATE, SSM_GROUPS * SSM_STATE))
    xs = xs.reshape(b, s, N_HEADS, HEAD_DIM)
    rep = N_HEADS // SSM_GROUPS
    bm = jnp.repeat(bm.reshape(b, s, SSM_GROUPS, SSM_STATE), rep, axis=2)
    cm = jnp.repeat(cm.reshape(b, s, SSM_GROUPS, SSM_STATE), rep, axis=2)
    dt = jax.nn.softplus((dt + dt_bias).astype(jnp.float32))
    a = -jnp.exp(a_log.astype(jnp.float32))
    y = chunked_decay_recurrence(cm, bm, xs * dt[..., None].astype(xs.dtype), dt * a, SSM_CHUNK)
    y = y + xs * d_skip[:, None]
    y = y.reshape(b, s, GROUP_WIDTH) * jax.nn.silu(z)
    return rms_norm(y, norm_w)


def dsa_group(q, k, v, iq, ik, iw, idx_k_norm, pos):
    b, s, _ = q.shape
    q = rope(q.reshape(b, s, N_HEADS, HEAD_DIM), pos)
    k = rope(k[:, :, None, :], pos)[:, :, 0]
    iq = rope(iq.reshape(b, s, IDX_HEADS, IDX_DIM), pos)
    ik = rope(rms_norm(ik, idx_k_norm)[:, :, None, :], pos)[:, :, 0]
    iw = iw * (IDX_HEADS ** -0.5 * IDX_DIM ** -0.5)
    topk = min(DSA_TOPK_MAX, s // 4)
    nb = s // DSA_QBLOCK
    scale = HEAD_DIM ** -0.5

    def to_blocks(t):
        return jnp.moveaxis(t.reshape((b, nb, DSA_QBLOCK) + t.shape[2:]), 1, 0)

    def block(args):
        q_b, iq_b, iw_b, t_b = args
        rel = jax.nn.relu(jnp.einsum('bqhd,bsd->bqhs', iq_b, ik))
        score = jnp.einsum('bqhs,bqh->bqs', rel, iw_b).astype(jnp.float32)
        causal = pos[None, :] <= t_b[:, None]
        score = jnp.where(causal[None], score, -jnp.inf)
        _, sel = lax.top_k(score, topk)
        k_sel = jax.vmap(lambda kb, ib: kb[ib])(k, sel)
        v_sel = jax.vmap(lambda vb, ib: vb[ib])(v, sel)
        logits = jnp.einsum('bqhd,bqkd->bqhk', q_b, k_sel).astype(jnp.float32) * scale
        valid = (sel <= t_b[None, :, None])[:, :, None, :]
        logits = jnp.where(valid, logits, -jnp.inf)
        p = jax.nn.softmax(logits, axis=-1).astype(v_sel.dtype)
        return jnp.einsum('bqhk,bqkd->bqhd', p, v_sel)

    out = lax.map(block, (to_blocks(q), to_blocks(iq), to_blocks(iw), pos.reshape(nb, DSA_QBLOCK)))
    return jnp.moveaxis(out, 0, 1).reshape(b, s, GROUP_WIDTH)


def cross_attention(h, memn, wq, wk, wv, wo):
    b, s, _ = h.shape
    m = memn.shape[1]
    q = (h @ wq).reshape(b, s, X_HEADS, X_HEAD_DIM)
    k = (memn @ wk).reshape(b, m, X_HEADS, X_HEAD_DIM)
    v = (memn @ wv).reshape(b, m, X_HEADS, X_HEAD_DIM)
    logits = jnp.einsum('bshd,bmhd->bhsm', q, k).astype(jnp.float32) * (X_HEAD_DIM ** -0.5)
    p = jax.nn.softmax(logits, axis=-1).astype(v.dtype)
    o = jnp.einsum('bhsm,bmhd->bshd', p, v).reshape(b, s, D_MODEL)
    return o @ wo


def conv_glu(h, w_up, conv_w, conv_b, w_down):
    gate, val = jnp.split(h @ w_up, 2, axis=-1)
    gate = jax.nn.silu(causal_dwconv(gate, conv_w, conv_b))
    return (gate * val) @ w_down


def setup_inputs(seed: int = 0) -> dict:
    key = jax.random.key(seed)
    ks = iter(jax.random.split(key, 48))
    L = DEPTH
    f32 = jnp.float32

    def nrm(shape, scale):
        return jax.random.normal(next(ks), shape, f32) * scale

    def gain(shape):
        return 1.0 + nrm(shape, 0.05)

    dt0 = jnp.exp(jax.random.uniform(next(ks), (L, N_HEADS), f32, math.log(1e-3), math.log(1e-1)))
    return {
        "x": nrm((BATCH, SEQ, D_MODEL), 1.0),
        "mem": nrm((BATCH, N_MEM, D_MODEL), 1.0),
        "norm_mix": gain((L, D_MODEL)),
        "w_in": nrm((L, D_MODEL, N_IN), D_MODEL ** -0.5),
        "rwkv_mu": jax.random.uniform(next(ks), (L, GROUP_COLS[1]), f32, 0.0, 1.0),
        "rwkv_w0": nrm((L, GROUP_WIDTH), 1.0) - 1.0,
        "rwkv_w2": nrm((L, RW_DECAY_LORA, GROUP_WIDTH), 0.5 * RW_DECAY_LORA ** -0.5),
        "rwkv_a0": nrm((L, GROUP_WIDTH), 0.1),
        "rwkv_a2": nrm((L, RW_A_LORA, GROUP_WIDTH), RW_A_LORA ** -0.5),
        "rwkv_g2": nrm((L, RW_GATE_LORA, GROUP_WIDTH), RW_GATE_LORA ** -0.5),
        "rwkv_k_k": 0.85 + nrm((L, GROUP_WIDTH), 0.05),
        "rwkv_k_a": gain((L, GROUP_WIDTH)),
        "rwkv_r_k": nrm((L, N_HEADS, HEAD_DIM), 0.1),
        "rwkv_ln_w": gain((L, GROUP_WIDTH)),
        "rwkv_ln_b": nrm((L, GROUP_WIDTH), 0.02),
        "ssm_conv_w": nrm((L, SSM_CONV, SSM_XBC), SSM_CONV ** -0.5),
        "ssm_conv_b": nrm((L, SSM_XBC), 0.02),
        "ssm_dt_bias": dt0 + jnp.log(-jnp.expm1(-dt0)),
        "ssm_a_log": jnp.log(jax.random.uniform(next(ks), (L, N_HEADS), f32, 1.0, 16.0)),
        "ssm_d": gain((L, N_HEADS)),
        "ssm_norm": gain((L, GROUP_WIDTH)),
        "idx_k_norm": gain((L, IDX_DIM)),
        "w_out": nrm((L, D_MIX, D_MODEL), D_MIX ** -0.5),
        "norm_cross": gain((L, D_MODEL)),
        "norm_mem": gain((L, D_MODEL)),
        "wq_x": nrm((L, D_MODEL, D_MODEL), D_MODEL ** -0.5),
        "wk_x": nrm((L, D_MODEL, D_MODEL), D_MODEL ** -0.5),
        "wv_x": nrm((L, D_MODEL, D_MODEL), D_MODEL ** -0.5),
        "wo_x": nrm((L, D_MODEL, D_MODEL), D_MODEL ** -0.5),
        "norm_ffn": gain((L, D_MODEL)),
        "w_up": nrm((L, D_MODEL, 2 * D_FF), D_MODEL ** -0.5),
        "ffn_conv_w": nrm((L, FFN_CONV, D_FF), FFN_CONV ** -0.5),
        "ffn_conv_b": nrm((L, D_FF), 0.02),
        "w_down": nrm((L, D_FF, D_MODEL), D_FF ** -0.5),
        "norm_final": gain((D_MODEL,)),
    }


def reference(x, mem, norm_mix, w_in, rwkv_mu, rwkv_w0, rwkv_w2, rwkv_a0, rwkv_a2, rwkv_g2,
              rwkv_k_k, rwkv_k_a, rwkv_r_k, rwkv_ln_w, rwkv_ln_b, ssm_conv_w, ssm_conv_b,
              ssm_dt_bias, ssm_a_log, ssm_d, ssm_norm, idx_k_norm, w_out, norm_cross, norm_mem,
              wq_x, wk_x, wv_x, wo_x, norm_ffn, w_up, ffn_conv_w, ffn_conv_b, w_down, norm_final):
    s = x.shape[1]
    pos = jnp.arange(s, dtype=jnp.int32)
    h = x
    for l in range(DEPTH):
        hn = rms_norm(h, norm_mix[l])
        proj = hn @ w_in[l]
        ret_p, rw_p, ssm_p, dsa_p = split_cols(proj, GROUP_COLS)
        rq, rk, rv, rg = split_cols(ret_p, RET_SPLITS)
        o_ret = retention_group(rq, rk, rv, rg, pos)
        rw_p = token_shift_mix(rw_p, rwkv_mu[l])
        wr, wk, wv, wwl, wal, wgl = split_cols(rw_p, RW_SPLITS)
        o_rw = rwkv7_group(wr, wk, wv, wwl, wal, wgl, rwkv_w0[l], rwkv_w2[l], rwkv_a0[l], rwkv_a2[l],
                           rwkv_g2[l], rwkv_k_k[l], rwkv_k_a[l], rwkv_r_k[l], rwkv_ln_w[l], rwkv_ln_b[l])
        sz, sxbc, sdt = split_cols(ssm_p, SSM_SPLITS)
        o_ssm = ssd_group(sz, sxbc, sdt, ssm_conv_w[l], ssm_conv_b[l], ssm_dt_bias[l], ssm_a_log[l],
                          ssm_d[l], ssm_norm[l])
        dq, dk, dv, diq, dik, diw = split_cols(dsa_p, DSA_SPLITS)
        o_dsa = dsa_group(dq, dk, dv, diq, dik, diw, idx_k_norm[l], pos)
        h = h + jnp.concatenate([o_ret, o_rw, o_ssm, o_dsa], axis=-1) @ w_out[l]
        memn = rms_norm(mem, norm_mem[l])
        h = h + cross_attention(rms_norm(h, norm_cross[l]), memn, wq_x[l], wk_x[l], wv_x[l], wo_x[l])
        h = h + conv_glu(rms_norm(h, norm_ffn[l]), w_up[l], ffn_conv_w[l], ffn_conv_b[l], w_down[l])
    return rms_norm(h, norm_final)
```

```python
import functools
import math

import jax
import jax.numpy as jnp
import numpy as np
from jax import lax
from jax.experimental import pallas as pl
from jax.experimental.pallas import tpu as pltpu

F32 = jnp.float32
BF16 = jnp.bfloat16
I32 = jnp.int32

D_MODEL = 1024
GROUP_WIDTH = 256
HEAD_DIM = 64
N_HEADS = 4
RET_QK_DIM = 32
RW_DECAY_SCALE = math.exp(-0.5)
RW_GN_EPS = 64e-5
SSM_STATE = 128
SSM_CONV = 4
SSM_XBC = 768
IDX_HEADS = 4
IDX_DIM = 32
DSA_TOPK_MAX = 256
DSA_QBLOCK = 128
X_HEADS = 4
X_HEAD_DIM = 256
D_FF = 2816
FFN_CONV = 3
ROPE_THETA = 10000.0
NORM_EPS = 1e-6

VMEM_LIMIT_BYTES = 52 * 1024 * 1024
SUBLANES = 8
LANES = 128

RET_CHUNK = 256
SSD_CHUNK = 128
RW_CHUNK = 64
DSA_KEY_CHUNK = 512
FF_CHUNK = 256


def _cparams(*sem):
    return pltpu.CompilerParams(dimension_semantics=sem, vmem_limit_bytes=VMEM_LIMIT_BYTES)


def _dot(a, b):
    return jnp.dot(a, b, preferred_element_type=F32)


def _dot_nt(a, b):
    return lax.dot_general(a, b, (((1,), (1,)), ((), ())), preferred_element_type=F32)


def _dot_tn(a, b):
    return lax.dot_general(a, b, (((0,), (0,)), ((), ())), preferred_element_type=F32)


def _split(a):
    hi = a.astype(BF16)
    lo = (a - hi.astype(F32)).astype(BF16)
    return hi, lo


def _dot_exact_rhs(a, b_bf16):
    hi, lo = _split(a)
    return _dot(hi, b_bf16) + _dot(lo, b_bf16)


def _dot3(a, b):
    ah, al = _split(a)
    bh, bl = _split(b)
    return _dot(ah, bh) + (_dot(ah, bl) + _dot(al, bh))


def _rms(x, g):
    return x * lax.rsqrt(jnp.mean(x * x, axis=-1, keepdims=True) + NORM_EPS) * g


def _silu(x):
    return x * jax.nn.sigmoid(x)


def _block_diag_ones(n, group_shift):
    r = lax.broadcasted_iota(I32, (n, n), 0) >> group_shift
    c = lax.broadcasted_iota(I32, (n, n), 1) >> group_shift
    return jnp.where(r == c, 1.0, 0.0).astype(BF16)


def _rot_half(x, half):
    n = x.shape[-1]
    lane = lax.broadcasted_iota(I32, x.shape, 1)
    first = (lane & (2 * half - 1)) < half
    return jnp.where(first, -pltpu.roll(x, n - half, 1), pltpu.roll(x, half, 1))


def _shift_rows(tail8, x, m):
    c = x.shape[0]
    xx = jnp.concatenate([tail8, x], axis=0)
    return pltpu.roll(xx, m, 0)[SUBLANES:SUBLANES + c]


def _cumsum_rows(x):
    c = x.shape[0]
    row = lax.broadcasted_iota(I32, x.shape, 0)
    d = 1
    while d < c:
        x = x + jnp.where(row >= d, pltpu.roll(x, d, 0), 0.0)
        d *= 2
    return x


def _head_select(cols, shift, width):
    c = cols[0].shape[0]
    lane_head = lax.broadcasted_iota(I32, (c, width), 1) >> shift
    out = jnp.zeros((c, width), F32)
    for h, col in enumerate(cols):
        out = jnp.where(lane_head == h, col, out)
    return out


def _norm_matmul_kernel(x_ref, g_ref, *refs, n_w):
    w_refs, o_refs = refs[:n_w], refs[n_w:]
    xb = _rms(x_ref[...], g_ref[...]).astype(BF16)
    for w_ref, o_ref in zip(w_refs, o_refs):
        o_ref[...] = _dot(xb, w_ref[...]).astype(o_ref.dtype)


def _norm_matmul(x2d, gain, weights, out_dtypes, tm):
    t, d = x2d.shape
    n_w = len(weights)
    in_specs = [pl.BlockSpec((tm, d), lambda i: (i, 0)), pl.BlockSpec((1, d), lambda i: (0, 0))]
    in_specs += [pl.BlockSpec(w.shape, lambda i: (0, 0)) for w in weights]
    out_specs = [pl.BlockSpec((tm, w.shape[1]), lambda i: (i, 0)) for w in weights]
    out_shape = [jax.ShapeDtypeStruct((t, w.shape[1]), dt) for w, dt in zip(weights, out_dtypes)]
    return pl.pallas_call(
        functools.partial(_norm_matmul_kernel, n_w=n_w),
        grid=(t // tm,), in_specs=in_specs, out_specs=out_specs, out_shape=out_shape,
        compiler_params=_cparams("parallel"), name="norm_matmul",
    )(x2d, gain.reshape(1, d), *weights)


def _ret_kernel(blk_ref, cos_ref, sin_ref, o_ref, state_ref, *, chunk):
    c = chunk

    @pl.when(pl.program_id(1) == 0)
    def _():
        state_ref[...] = jnp.zeros_like(state_ref)

    cos, sin = cos_ref[...], sin_ref[...]
    q = blk_ref[:, 0:128]
    k = blk_ref[:, 128:256]
    v = blk_ref[:, 256:512]
    g = blk_ref[:, 512:768]
    q = q * cos + _rot_half(q, RET_QK_DIM // 2) * sin
    k = (k * cos + _rot_half(k, RET_QK_DIM // 2) * sin) * (RET_QK_DIM ** -0.5)

    log_gamma = [math.log(1.0 - 2.0 ** (-5.0 - h)) for h in range(N_HEADS)]
    lane_k = lax.broadcasted_iota(I32, (1, 128), 1) >> 5
    lg_lane = jnp.zeros((1, 128), F32)
    for h in range(N_HEADS):
        lg_lane = jnp.where(lane_k == h, log_gamma[h], lg_lane)
    t_col = lax.broadcasted_iota(I32, (c, 1), 0).astype(F32)
    qd = q * jnp.exp((t_col + 1.0) * lg_lane)
    kd = k * jnp.exp((float(c - 1) - t_col) * lg_lane)

    ti = lax.broadcasted_iota(I32, (c, c), 0)
    si = lax.broadcasted_iota(I32, (c, c), 1)
    causal = ti >= si
    diff = jnp.where(causal, ti - si, 0).astype(F32)
    lane_qk = lax.broadcasted_iota(I32, (c, 128), 1) >> 5
    lane_v = lax.broadcasted_iota(I32, (c, 256), 1) >> 6
    kb = k.astype(BF16)
    vb = v.astype(BF16)
    y = _dot(qd.astype(BF16), state_ref[...].astype(BF16))
    for h in range(N_HEADS):
        qm = jnp.where(lane_qk == h, q, 0.0).astype(BF16)
        s = _dot_nt(qm, kb)
        dec = jnp.where(causal, jnp.exp(diff * log_gamma[h]), 0.0)
        yh = _dot((s * dec).astype(BF16), vb)
        y = y + jnp.where(lane_v == h, yh, 0.0)

    row_h = lax.broadcasted_iota(I32, (128, 256), 0) >> 5
    col_h = lax.broadcasted_iota(I32, (128, 256), 1) >> 6
    lg_row = jnp.zeros((128, 256), F32)
    for h in range(N_HEADS):
        lg_row = jnp.where(row_h == h, log_gamma[h], lg_row)
    new = _dot_tn(kd.astype(BF16), vb)
    state_ref[...] = state_ref[...] * jnp.exp(float(c) * lg_row) + jnp.where(row_h == col_h, new, 0.0)

    ms = _dot_exact_rhs(y * y, _block_diag_ones(256, 6)) * (1.0 / HEAD_DIM)
    o_ref[...] = y * lax.rsqrt(ms + NORM_EPS) * _silu(g)


def _retention(ret_p, cos32, sin32):
    b, s, _ = ret_p.shape
    c = min(RET_CHUNK, s)
    return pl.pallas_call(
        functools.partial(_ret_kernel, chunk=c),
        grid=(b, s // c),
        in_specs=[pl.BlockSpec((None, c, 768), lambda i, j: (i, j, 0)),
                  pl.BlockSpec((c, 128), lambda i, j: (j, 0)),
                  pl.BlockSpec((c, 128), lambda i, j: (j, 0))],
        out_specs=pl.BlockSpec((None, c, 256), lambda i, j: (i, j, 0)),
        out_shape=jax.ShapeDtypeStruct((b, s, 256), F32),
        scratch_shapes=[pltpu.VMEM((128, 256), F32)],
        compiler_params=_cparams("parallel", "arbitrary"), name="retention",
    )(ret_p, cos32, sin32)


def _ssd_kernel(blk_ref, cw_ref, cb_ref, dtb_ref, aneg_ref, dsk_ref, nw_ref, o_ref,
                state_ref, tail_ref, *, chunk):
    c = chunk

    @pl.when(pl.program_id(1) == 0)
    def _():
        state_ref[...] = jnp.zeros_like(state_ref)
        tail_ref[...] = jnp.zeros_like(tail_ref)

    z = blk_ref[:, 0:256]
    xbc = blk_ref[:, 256:1024]
    dt_raw = blk_ref[:, 1024:1152]

    tail = tail_ref[...]
    conv = xbc * cw_ref[SSM_CONV - 1:SSM_CONV, :] + cb_ref[...]
    for m in range(1, SSM_CONV):
        conv = conv + _shift_rows(tail, xbc, m) * cw_ref[SSM_CONV - 1 - m:SSM_CONV - m, :]
    tail_ref[...] = xbc[c - SUBLANES:c, :]
    xbc = _silu(conv)
    xs = xbc[:, 0:256]
    bm = xbc[:, 256:512]
    cm = xbc[:, 512:768]

    u = dt_raw + dtb_ref[...]
    dt = jnp.maximum(u, 0.0) + jnp.log1p(jnp.exp(-jnp.abs(u)))
    la = dt * aneg_ref[...]
    cum = _cumsum_rows(la)
    cum_t = cum.T
    cum_last = cum[c - 1:c, :]

    dt_lane = _head_select([dt[:, h:h + 1] for h in range(N_HEADS)], 6, 256)
    cum_lane = _head_select([cum[:, h:h + 1] for h in range(N_HEADS)], 6, 256)
    end_lane = _head_select([jnp.broadcast_to(cum_last[:, h:h + 1], (c, 1)) for h in range(N_HEADS)], 6, 256)
    xdt = xs * dt_lane
    x_end = (xdt * jnp.exp(end_lane - cum_lane)).astype(BF16)
    xdt_b = xdt.astype(BF16)
    e_lane = jnp.exp(cum_lane)
    chunk_decay = jnp.exp(end_lane[0:1, :])

    ti = lax.broadcasted_iota(I32, (c, c), 0)
    si = lax.broadcasted_iota(I32, (c, c), 1)
    causal = ti >= si
    lane_half = lax.broadcasted_iota(I32, (c, 128), 1) >> 6
    ys = []
    for grp in range(2):
        sl = slice(128 * grp, 128 * grp + 128)
        cg = cm[:, sl].astype(BF16)
        bg = bm[:, sl].astype(BF16)
        s = _dot_nt(cg, bg)
        parts = []
        for hh in range(2):
            h = 2 * grp + hh
            seg = jnp.minimum(cum[:, h:h + 1] - cum_t[h:h + 1, :], 0.0)
            dec = jnp.where(causal, jnp.exp(seg), 0.0)
            parts.append(_dot((s * dec).astype(BF16), xdt_b[:, sl]))
        y_g = jnp.where(lane_half == 0, parts[0], parts[1])
        st = state_ref[grp]
        y_g = y_g + e_lane[:, sl] * _dot(cg, st.astype(BF16))
        state_ref[grp] = st * chunk_decay[:, sl] + _dot_tn(bg, x_end[:, sl])
        ys.append(y_g)
    y = jnp.concatenate(ys, axis=1)
    y = (y + xs * dsk_ref[...]) * _silu(z)
    o_ref[...] = _rms(y, nw_ref[...])


def _ssd(ssm_p, conv_w, conv_b, dt_bias, a_log, d_skip, norm_w):
    b, s, _ = ssm_p.shape
    c = min(SSD_CHUNK, s)
    pad4 = lambda v: jnp.pad(v.astype(F32), (0, 128 - N_HEADS)).reshape(1, 128)
    params = [conv_w.astype(F32), conv_b.reshape(1, SSM_XBC), pad4(dt_bias),
              pad4(-jnp.exp(a_log.astype(F32))),
              jnp.repeat(d_skip, HEAD_DIM).reshape(1, 256), norm_w.reshape(1, 256)]
    full = lambda a: pl.BlockSpec(a.shape, lambda i, j: (0,) * a.ndim)
    return pl.pallas_call(
        functools.partial(_ssd_kernel, chunk=c),
        grid=(b, s // c),
        in_specs=[pl.BlockSpec((None, c, 1152), lambda i, j: (i, j, 0))] + [full(p) for p in params],
        out_specs=pl.BlockSpec((None, c, 256), lambda i, j: (i, j, 0)),
        out_shape=jax.ShapeDtypeStruct((b, s, 256), F32),
        scratch_shapes=[pltpu.VMEM((2, 128, 128), F32), pltpu.VMEM((SUBLANES, SSM_XBC), F32)],
        compiler_params=_cparams("parallel", "arbitrary"), name="ssd",
    )(ssm_p, *params)


def _rwkv_kernel(blk_ref, mu_ref, w0_ref, w2_ref, a0_ref, a2_ref, g2_ref, kk_ref, ka_ref, rk_ref,
                 lnw_ref, lnb_ref, o_ref, state_ref, prev_ref, *, chunk):
    c = chunk
    n = N_HEADS * c

    @pl.when(pl.program_id(1) == 0)
    def _():
        state_ref[...] = jnp.zeros_like(state_ref)
        prev_ref[...] = jnp.zeros_like(prev_ref)

    f = blk_ref[...]
    prev = _shift_rows(prev_ref[...], f, 1)
    prev_ref[...] = f[c - SUBLANES:c, :]
    x = f + (prev - f) * mu_ref[...]
    r = x[:, 0:256]
    k = x[:, 256:512]
    v = x[:, 512:768]
    lora = x[:, 768:896]
    gl = x[:, 896:1024]

    logw = -RW_DECAY_SCALE * jax.nn.sigmoid(w0_ref[...] + _dot3(jnp.tanh(lora), w2_ref[...]))
    a = jax.nn.sigmoid(a0_ref[...] + _dot3(lora, a2_ref[...]))
    gate = _dot(jax.nn.sigmoid(gl).astype(BF16), g2_ref[...].astype(BF16))
    bd = _block_diag_ones(256, 6)
    kk = k * kk_ref[...]
    kk = kk * lax.rsqrt(_dot_exact_rhs(kk * kk, bd) + 1e-12)
    k2 = k * (1.0 + (a - 1.0) * ka_ref[...])

    cum = _cumsum_rows(logw)
    cum_end = cum[c - 1:c, :]
    g_inc = jnp.exp(cum)
    g_exc = jnp.exp(cum - logw)
    g_inv = jnp.exp(-cum)
    g_end = jnp.exp(cum_end - cum)

    lane_h = lax.broadcasted_iota(I32, (c, 256), 1) >> 6

    def stack(t):
        return jnp.concatenate([jnp.where(lane_h == h, t, 0.0) for h in range(N_HEADS)], axis=0).astype(BF16)

    a_s = stack(-kk * g_exc)
    b_s = stack(kk * a * g_inv)
    k_s = stack(k2 * g_inv)
    r_s = stack(r * g_inc)
    v_s = stack(v)
    b_end = stack(kk * a * g_end)
    k_end = stack(k2 * g_end)

    row = lax.broadcasted_iota(I32, (n, n), 0)
    col = lax.broadcasted_iota(I32, (n, n), 1)
    strict = row > col
    incl = row >= col
    l_ab = jnp.where(strict, _dot_nt(a_s, b_s), 0.0)
    l_ak = jnp.where(strict, _dot_nt(a_s, k_s), 0.0).astype(BF16)
    p_rb = jnp.where(incl, _dot_nt(r_s, b_s), 0.0).astype(BF16)
    p_rk = jnp.where(incl, _dot_nt(r_s, k_s), 0.0).astype(BF16)

    m0 = state_ref[...]
    m0b = m0.astype(BF16)
    u = _dot(a_s, m0b) + _dot(l_ak, v_s)
    p = l_ab
    steps = max(1, int(math.ceil(math.log2(c))))
    for i in range(steps):
        u = u + _dot3(p, u)
        if i + 1 < steps:
            p = _dot3(p, p)
    ub = u.astype(BF16)
    y_s = _dot(r_s, m0b) + _dot(p_rb, ub) + _dot(p_rk, v_s)
    y = y_s[0:c]
    for h in range(1, N_HEADS):
        y = y + y_s[h * c:(h + 1) * c]

    ones = jnp.ones((c, 128), BF16)
    lw_hi, lw_lo = _split(logw)
    lw_lo2 = (logw - lw_hi.astype(F32) - lw_lo.astype(F32)).astype(BF16)
    tot = _dot_tn(lw_hi, ones) + _dot_tn(lw_lo, ones) + _dot_tn(lw_lo2, ones)
    g_col = jnp.exp(jnp.concatenate([tot, tot], axis=1))
    state_ref[...] = m0 * g_col + _dot_tn(b_end, ub) + _dot_tn(k_end, v_s)

    inv_n = 1.0 / HEAD_DIM
    mean = _dot_exact_rhs(y, bd) * inv_n
    yc = y - mean
    var = _dot_exact_rhs(yc * yc, bd) * inv_n
    yn = yc * lax.rsqrt(var + RW_GN_EPS) * lnw_ref[...] + lnb_ref[...]
    bonus = _dot_exact_rhs(r * k2 * rk_ref[...], bd) * v
    o_ref[...] = (yn + bonus) * gate


def _rwkv(rw_p, mu, w0, w2, a0, a2, g2, k_k, k_a, r_k, ln_w, ln_b):
    b, s, _ = rw_p.shape
    c = min(RW_CHUNK, s)
    row = lambda v: v.astype(F32).reshape(1, -1)
    w2p = jnp.concatenate([w2, jnp.zeros_like(w2)], axis=0)
    a2p = jnp.concatenate([jnp.zeros_like(a2), a2], axis=0)
    params = [row(mu), row(w0), w2p, row(a0), a2p, g2, row(k_k), row(k_a), row(r_k), row(ln_w), row(ln_b)]
    full = lambda a: pl.BlockSpec(a.shape, lambda i, j: (0,) * a.ndim)
    return pl.pallas_call(
        functools.partial(_rwkv_kernel, chunk=c),
        grid=(b, s // c),
        in_specs=[pl.BlockSpec((None, c, 1024), lambda i, j: (i, j, 0))] + [full(p) for p in params],
        out_specs=pl.BlockSpec((None, c, 256), lambda i, j: (i, j, 0)),
        out_shape=jax.ShapeDtypeStruct((b, s, 256), F32),
        scratch_shapes=[pltpu.VMEM((256, 256), F32), pltpu.VMEM((SUBLANES, 1024), F32)],
        compiler_params=_cparams("parallel", "arbitrary"), name="rwkv7",
    )(rw_p, *params)


def _dsa_kernel(q_ref, iq_ref, kv_ref, iki_ref, cos32_ref, sin32_ref, cos64_ref, sin64_ref, g_ref,
                o_ref, ik4_ref, ik4l_ref, k4_ref, v4_ref, keys_ref, m_ref, l_ref, acc_ref, *, seq, topk, kc):
    qb = DSA_QBLOCK
    j = pl.program_id(1)
    n_kc = (j * qb + qb + kc - 1) // kc

    @pl.when(j == 0)
    def _():
        lane = lax.broadcasted_iota(I32, (seq, 128), 1)
        kv = kv_ref[...]
        c64, s64 = cos64_ref[:, 0:128], sin64_ref[:, 0:128]
        kr = kv * c64 + _rot_half(kv, HEAD_DIM // 2) * s64
        kz = jnp.where(lane < HEAD_DIM, kr, 0.0)
        k2 = (kz + pltpu.roll(kz, 64, 1)).astype(BF16)
        k4_ref[...] = jnp.concatenate([k2, k2], axis=1)
        vz = jnp.where(lane >= HEAD_DIM, kv, 0.0)
        v2 = (vz + pltpu.roll(vz, 64, 1)).astype(BF16)
        v4_ref[...] = jnp.concatenate([v2, v2], axis=1)
        ik = jnp.where(lane < IDX_DIM, iki_ref[...], 0.0)
        ik = ik * lax.rsqrt(jnp.sum(ik * ik, axis=-1, keepdims=True) * (1.0 / IDX_DIM) + NORM_EPS) * g_ref[...]
        ik = ik * cos32_ref[...] + _rot_half(ik, IDX_DIM // 2) * sin32_ref[...]
        ik = jnp.where(lane < IDX_DIM, ik, 0.0)
        ik = ik + pltpu.roll(ik, 32, 1)
        ik4_ref[...], ik4l_ref[...] = _split(ik + pltpu.roll(ik, 64, 1))

    q0 = pl.multiple_of(j * qb, qb)
    q_pos = q0 + lax.broadcasted_iota(I32, (qb, 1), 0)

    iq = iq_ref[...]
    iq = iq * cos32_ref[pl.ds(q0, qb), :] + _rot_half(iq, IDX_DIM // 2) * sin32_ref[pl.ds(q0, qb), :]
    lane_i = lax.broadcasted_iota(I32, (qb, 128), 1) >> 5
    iq_h, iq_l = _split(jnp.concatenate([jnp.where(lane_i == h, iq, 0.0) for h in range(IDX_HEADS)], axis=0))
    iw = iki_ref[pl.ds(q0, qb), :] * (IDX_HEADS ** -0.5 * IDX_DIM ** -0.5)
    iw_cols = [iw[:, IDX_DIM + h:IDX_DIM + h + 1] for h in range(IDX_HEADS)]
    col_iota = lax.broadcasted_iota(I32, (qb, kc), 1)

    def score_body(ci, carry):
        off = pl.multiple_of(ci * kc, kc)
        ik_h = ik4_ref[pl.ds(off, kc), :]
        dots = _dot_nt(iq_h, ik_h) + (_dot_nt(iq_l, ik_h) + _dot_nt(iq_h, ik4l_ref[pl.ds(off, kc), :]))
        rel = jnp.maximum(dots, 0.0)
        sc = rel[0:qb] * iw_cols[0]
        for h in range(1, IDX_HEADS):
            sc = sc + rel[h * qb:(h + 1) * qb] * iw_cols[h]
        sc = sc + 0.0
        sc = jnp.where(off + col_iota <= q_pos, sc, -jnp.inf)
        bits = lax.bitcast_convert_type(sc, I32)
        keys_ref[:, pl.ds(off, kc)] = bits ^ ((bits >> 31) & 0x7FFFFFFF)
        return carry

    lax.fori_loop(0, n_kc, score_body, 0)

    def count(pred):
        def body(ci, acc):
            off = pl.multiple_of(ci * kc, kc)
            x = jnp.where(pred(keys_ref[:, pl.ds(off, kc)], off), 1.0, 0.0)
            part = x[:, 0:128]
            for t in range(1, kc // 128):
                part = part + x[:, t * 128:(t + 1) * 128]
            return acc + part
        acc = lax.fori_loop(0, n_kc, body, jnp.zeros((qb, 128), F32))
        return jnp.sum(acc, axis=1, keepdims=True)

    int_min = -(2 ** 31)

    def bit_body(i, t_u):
        cand_u = t_u | lax.shift_left(jnp.int32(1), jnp.int32(31) - i)
        cand_s = cand_u ^ int_min
        cnt = count(lambda kk_, off: kk_ >= cand_s)
        return jnp.where(cnt >= float(topk), cand_u, t_u)

    t_u = lax.fori_loop(0, 32, bit_body, jnp.zeros((qb, 1), I32))
    tau = t_u ^ int_min
    need = float(topk) - count(lambda kk_, off: kk_ > tau)

    idx_bits = max(1, int(math.ceil(math.log2(seq))))

    def idx_body(i, cst):
        cand = cst | lax.shift_left(jnp.int32(1), jnp.int32(idx_bits - 1) - i)
        cnt = count(lambda kk_, off: (kk_ == tau) & (off + col_iota < cand))
        return jnp.where(cnt < need, cand, cst)

    c_star = lax.fori_loop(0, idx_bits, idx_body, jnp.zeros((qb, 1), I32))

    q = q_ref[...]
    q = (q * cos64_ref[pl.ds(q0, qb), :] + _rot_half(q, HEAD_DIM // 2) * sin64_ref[pl.ds(q0, qb), :]) * (HEAD_DIM ** -0.5)
    lane_q = lax.broadcasted_iota(I32, (qb, 256), 1) >> 6
    q_s = jnp.concatenate([jnp.where(lane_q == h, q, 0.0) for h in range(N_HEADS)], axis=0).astype(BF16)
    neg = -1e30
    m_ref[...] = jnp.full_like(m_ref, neg)
    l_ref[...] = jnp.zeros_like(l_ref)
    acc_ref[...] = jnp.zeros_like(acc_ref)

    def attn_body(ci, carry):
        off = pl.multiple_of(ci * kc, kc)
        kk_ = keys_ref[:, pl.ds(off, kc)]
        colp = off + col_iota
        sel = ((kk_ > tau) | ((kk_ == tau) & (colp <= c_star))) & (colp <= q_pos)
        sel_i = jnp.where(sel, 1, 0).astype(I32)
        sel4 = jnp.concatenate([sel_i] * N_HEADS, axis=0) > 0
        s = jnp.where(sel4, _dot_nt(q_s, k4_ref[pl.ds(off, kc), :]), neg)
        m_old = m_ref[...]
        m_new = jnp.maximum(m_old, jnp.max(s, axis=-1, keepdims=True))
        alpha = jnp.exp(m_old - m_new)
        p = jnp.where(sel4, jnp.exp(s - m_new), 0.0)
        l_ref[...] = alpha * l_ref[...] + jnp.sum(p, axis=-1, keepdims=True)
        acc_ref[...] = alpha * acc_ref[...] + _dot(p.astype(BF16), v4_ref[pl.ds(off, kc), :])
        m_ref[...] = m_new
        return carry

    lax.fori_loop(0, n_kc, attn_body, 0)
    o_s = acc_ref[...] / l_ref[...]
    o = jnp.where(lane_q == 0, o_s[0:qb], 0.0)
    for h in range(1, N_HEADS):
        o = jnp.where(lane_q == h, o_s[h * qb:(h + 1) * qb], o)
    o_ref[...] = o


def _dsa(dsa_p, idx_k_norm, cos32, sin32, cos64, sin64):
    b, s, _ = dsa_p.shape
    qb = DSA_QBLOCK
    kc = min(DSA_KEY_CHUNK, s)
    topk = min(DSA_TOPK_MAX, s // 4)
    gpad = jnp.pad(idx_k_norm.astype(F32), (0, 128 - IDX_DIM)).reshape(1, 128)
    full = lambda a: pl.BlockSpec(a.shape, lambda i, j: (0,) * a.ndim)
    return pl.pallas_call(
        functools.partial(_dsa_kernel, seq=s, topk=topk, kc=kc),
        grid=(b, s // qb),
        in_specs=[pl.BlockSpec((None, qb, 256), lambda i, j: (i, j, 0)),
                  pl.BlockSpec((None, qb, 128), lambda i, j: (i, j, 2)),
                  pl.BlockSpec((None, s, 128), lambda i, j: (i, 0, 3)),
                  pl.BlockSpec((None, s, 128), lambda i, j: (i, 0, 4)),
                  full(cos32), full(sin32), full(cos64), full(sin64), full(gpad)],
        out_specs=pl.BlockSpec((None, qb, 256), lambda i, j: (i, j, 0)),
        out_shape=jax.ShapeDtypeStruct((b, s, 256), F32),
        scratch_shapes=[pltpu.VMEM((s, 128), BF16), pltpu.VMEM((s, 128), BF16),
                        pltpu.VMEM((s, 256), BF16), pltpu.VMEM((s, 256), BF16),
                        pltpu.VMEM((qb, s), I32), pltpu.VMEM((N_HEADS * qb, 1), F32),
                        pltpu.VMEM((N_HEADS * qb, 1), F32), pltpu.VMEM((N_HEADS * qb, 256), F32)],
        compiler_params=_cparams("parallel", "arbitrary"), name="dsa",
    )(dsa_p, dsa_p, dsa_p, dsa_p, cos32, sin32, cos64, sin64, gpad)


def _mix_cross_kernel(h_ref, o0_ref, o1_ref, o2_ref, o3_ref, wout_ref, g_ref, wq_ref, k_ref, v_ref, wo_ref,
                      out_ref):
    h = h_ref[...]
    for i, o_ref in enumerate((o0_ref, o1_ref, o2_ref, o3_ref)):
        h = h + _dot(o_ref[...].astype(BF16), wout_ref[i * GROUP_WIDTH:(i + 1) * GROUP_WIDTH, :])
    q = _dot(_rms(h, g_ref[...]).astype(BF16), wq_ref[...]) * (X_HEAD_DIM ** -0.5)
    outs = []
    for hd in range(X_HEADS):
        sl = slice(hd * X_HEAD_DIM, (hd + 1) * X_HEAD_DIM)
        s = _dot_nt(q[:, sl].astype(BF16), k_ref[:, sl])
        p = jnp.exp(s - jnp.max(s, axis=-1, keepdims=True))
        p = p / jnp.sum(p, axis=-1, keepdims=True)
        outs.append(_dot(p.astype(BF16), v_ref[:, sl]))
    o = jnp.concatenate(outs, axis=1).astype(BF16)
    out_ref[...] = h + _dot(o, wo_ref[...])


def _mix_cross(h2d, outs, w_out, g_cross, wq, k_mem, v_mem, wo, seq, tm):
    t, d = h2d.shape
    n_mem = k_mem.shape[1]
    per_seq = seq // tm
    tok = lambda w: pl.BlockSpec((tm, w), lambda i: (i, 0))
    const = lambda a: pl.BlockSpec(a.shape, lambda i: (0,) * a.ndim)
    mem = pl.BlockSpec((None, n_mem, d), lambda i: (i // per_seq, 0, 0))
    return pl.pallas_call(
        _mix_cross_kernel,
        grid=(t // tm,),
        in_specs=[tok(d)] + [tok(GROUP_WIDTH)] * 4 + [const(w_out), pl.BlockSpec((1, d), lambda i: (0, 0)),
                                                     const(wq), mem, mem, const(wo)],
        out_specs=tok(d),
        out_shape=jax.ShapeDtypeStruct((t, d), F32),
        compiler_params=_cparams("parallel"), name="mix_cross",
    )(h2d, *outs, w_out, g_cross.reshape(1, d), wq, k_mem, v_mem, wo)


def _ffn_kernel(h_ref, g_ref, wg_ref, wv_ref, cw_ref, cb_ref, wd_ref, gf_ref, out_ref, tail_ref, acc_ref,
                *, per_seq, n_chunks, final_norm):
    tm = h_ref.shape[0]

    @pl.when(pl.program_id(0) % per_seq == 0)
    def _():
        tail_ref[...] = jnp.zeros_like(tail_ref)

    h = h_ref[...]
    xb = _rms(h, g_ref[...]).astype(BF16)
    acc_ref[...] = jnp.zeros_like(acc_ref)

    def body(ci, carry):
        gate = _dot(xb, wg_ref[ci])
        val = _dot(xb, wv_ref[ci])
        tail = tail_ref[ci]
        cw = cw_ref[ci]
        conv = gate * cw[FFN_CONV - 1:FFN_CONV, :] + cb_ref[ci]
        for m in range(1, FFN_CONV):
            conv = conv + _shift_rows(tail, gate, m) * cw[FFN_CONV - 1 - m:FFN_CONV - m, :]
        tail_ref[ci] = gate[tm - SUBLANES:tm, :]
        acc_ref[...] += _dot((_silu(conv) * val).astype(BF16), wd_ref[ci])
        return carry

    lax.fori_loop(0, n_chunks, body, 0)
    out = h + acc_ref[...]
    if final_norm:
        out = _rms(out, gf_ref[...])
    out_ref[...] = out


def _ffn(h2d, g_ffn, w_up, conv_w, conv_b, w_down, g_final, seq, tm, final_norm):
    t, d = h2d.shape
    fc = FF_CHUNK
    nck = D_FF // fc
    chunked = lambda w: w.reshape(w.shape[0], nck, fc).transpose(1, 0, 2)
    wg = chunked(w_up[:, :D_FF]).astype(BF16)
    wv = chunked(w_up[:, D_FF:]).astype(BF16)
    cw = chunked(conv_w.astype(F32))
    cb = conv_b.astype(F32).reshape(nck, 1, fc)
    wd = w_down.reshape(nck, fc, d).astype(BF16)
    const = lambda a: pl.BlockSpec(a.shape, lambda i: (0,) * a.ndim)
    row = pl.BlockSpec((1, d), lambda i: (0, 0))
    return pl.pallas_call(
        functools.partial(_ffn_kernel, per_seq=seq // tm, n_chunks=nck, final_norm=final_norm),
        grid=(t // tm,),
        in_specs=[pl.BlockSpec((tm, d), lambda i: (i, 0)), row, const(wg), const(wv), const(cw), const(cb),
                  const(wd), row],
        out_specs=pl.BlockSpec((tm, d), lambda i: (i, 0)),
        out_shape=jax.ShapeDtypeStruct((t, d), F32),
        scratch_shapes=[pltpu.VMEM((nck, SUBLANES, fc), F32), pltpu.VMEM((tm, d), F32)],
        compiler_params=_cparams("arbitrary"), name="conv_glu",
    )(h2d, g_ffn.reshape(1, d), wg, wv, cw, cb, wd, g_final.reshape(1, d))


def _rope_tables(seq, head_dim, width):
    half = head_dim // 2
    inv = ROPE_THETA ** (-jnp.arange(half, dtype=F32) / half)
    ang = jnp.arange(seq, dtype=F32)[:, None] * inv[None, :]
    reps = width // half
    return jnp.tile(jnp.cos(ang), (1, reps)), jnp.tile(jnp.sin(ang), (1, reps))


def _split_w_in(w):
    ret = w[:, 0:768]
    rw = w[:, 768:1792]
    ssm = jnp.pad(w[:, 1792:2820], ((0, 0), (0, 1152 - 1028)))
    d = w[:, 2820:3368]
    dsa = jnp.concatenate([d[:, 0:256], d[:, 384:512], d[:, 256:384], d[:, 512:548],
                           jnp.zeros((w.shape[0], 128 - 36), w.dtype)], axis=1)
    return [t.astype(BF16) for t in (ret, rw, ssm, dsa)]


def kernel(x, mem, norm_mix, w_in, rwkv_mu, rwkv_w0, rwkv_w2, rwkv_a0, rwkv_a2, rwkv_g2, rwkv_k_k, rwkv_k_a, rwkv_r_k, rwkv_ln_w, rwkv_ln_b, ssm_conv_w, ssm_conv_b, ssm_dt_bias, ssm_a_log, ssm_d, ssm_norm, idx_k_norm, w_out, norm_cross, norm_mem, wq_x, wk_x, wv_x, wo_x, norm_ffn, w_up, ffn_conv_w, ffn_conv_b, w_down, norm_final):
    b, s, d = x.shape
    n_mem = mem.shape[1]
    depth = w_in.shape[0]
    t = b * s
    tm = min(512, s)
    cos32, sin32 = _rope_tables(s, RET_QK_DIM, 128)
    cos64, sin64 = _rope_tables(s, HEAD_DIM, 256)
    mem2d = mem.reshape(b * n_mem, d)
    h = x.reshape(t, d)
    for l in range(depth):
        ret_p, rw_p, ssm_p, dsa_p = _norm_matmul(h, norm_mix[l], _split_w_in(w_in[l]), [F32] * 4, tm)
        o_ret = _retention(ret_p.reshape(b, s, -1), cos32, sin32)
        o_rw = _rwkv(rw_p.reshape(b, s, -1), rwkv_mu[l], rwkv_w0[l], rwkv_w2[l], rwkv_a0[l], rwkv_a2[l],
                     rwkv_g2[l], rwkv_k_k[l], rwkv_k_a[l], rwkv_r_k[l], rwkv_ln_w[l], rwkv_ln_b[l])
        o_ssm = _ssd(ssm_p.reshape(b, s, -1), ssm_conv_w[l], ssm_conv_b[l], ssm_dt_bias[l], ssm_a_log[l],
                     ssm_d[l], ssm_norm[l])
        o_dsa = _dsa(dsa_p.reshape(b, s, -1), idx_k_norm[l], cos32, sin32, cos64, sin64)
        k_mem, v_mem = _norm_matmul(mem2d, norm_mem[l], [wk_x[l].astype(BF16), wv_x[l].astype(BF16)],
                                    [BF16, BF16], min(512, b * n_mem))
        outs = [o.reshape(t, GROUP_WIDTH) for o in (o_ret, o_rw, o_ssm, o_dsa)]
        h = _mix_cross(h, outs, w_out[l].astype(BF16), norm_cross[l], wq_x[l].astype(BF16),
                       k_mem.reshape(b, n_mem, d), v_mem.reshape(b, n_mem, d), wo_x[l].astype(BF16), s, tm)
        h = _ffn(h, norm_ffn[l], w_up[l], ffn_conv_w[l], ffn_conv_b[l], w_down[l], norm_final, s, tm,
                 final_norm=(l == depth - 1))
    return h.reshape(b, s, d)
```

```python
import functools
import math

import jax
import jax.numpy as jnp
import numpy as np
from jax import lax
from jax.experimental import pallas as pl
from jax.experimental.pallas import tpu as pltpu

F32 = jnp.float32
BF16 = jnp.bfloat16
I32 = jnp.int32
I16 = jnp.int16

D_MODEL = 1024
GROUP_WIDTH = 256
HEAD_DIM = 64
N_HEADS = 4
RET_QK_DIM = 32
RW_DECAY_SCALE = math.exp(-0.5)
RW_GN_EPS = 64e-5
SSM_STATE = 128
SSM_CONV = 4
SSM_XBC = 768
IDX_HEADS = 4
IDX_DIM = 32
DSA_TOPK_MAX = 256
DSA_QBLOCK = 128
X_HEADS = 4
X_HEAD_DIM = 256
D_FF = 2816
FFN_CONV = 3
ROPE_THETA = 10000.0
NORM_EPS = 1e-6

VMEM_LIMIT_BYTES = 52 * 1024 * 1024
SUBLANES = 8
LANES = 128

RET_CHUNK = 256
SSD_CHUNK = 128
RW_CHUNK = 64
RW_SEQS_PER_STEP = 2
DSA_KEY_CHUNK = 512
FF_CHUNK = 256


def _cparams(*sem):
    return pltpu.CompilerParams(dimension_semantics=sem, vmem_limit_bytes=VMEM_LIMIT_BYTES)


def _dot(a, b):
    return jnp.dot(a, b, preferred_element_type=F32)


def _dot_nt(a, b):
    return lax.dot_general(a, b, (((1,), (1,)), ((), ())), preferred_element_type=F32)


def _dot_tn(a, b):
    return lax.dot_general(a, b, (((0,), (0,)), ((), ())), preferred_element_type=F32)


def _split(a):
    hi = a.astype(BF16)
    lo = (a - hi.astype(F32)).astype(BF16)
    return hi, lo


def _dot_exact_rhs(a, b_bf16):
    hi, lo = _split(a)
    return _dot(hi, b_bf16) + _dot(lo, b_bf16)


def _dot3(a, b):
    ah, al = _split(a)
    bh, bl = _split(b)
    return _dot(ah, bh) + (_dot(ah, bl) + _dot(al, bh))


def _rms(x, g):
    return x * lax.rsqrt(jnp.mean(x * x, axis=-1, keepdims=True) + NORM_EPS) * g


def _silu(x):
    return x * jax.nn.sigmoid(x)


def _block_diag_ones(n, group_shift):
    r = lax.broadcasted_iota(I32, (n, n), 0) >> group_shift
    c = lax.broadcasted_iota(I32, (n, n), 1) >> group_shift
    return jnp.where(r == c, 1.0, 0.0).astype(BF16)


def _rot_half(x, half):
    n = x.shape[-1]
    lane = lax.broadcasted_iota(I32, x.shape, 1)
    first = (lane & (2 * half - 1)) < half
    return jnp.where(first, -pltpu.roll(x, n - half, 1), pltpu.roll(x, half, 1))


def _shift_rows(tail8, x, m):
    c = x.shape[0]
    xx = jnp.concatenate([tail8, x], axis=0)
    return pltpu.roll(xx, m, 0)[SUBLANES:SUBLANES + c]


def _cumsum_rows(x):
    c = x.shape[0]
    row = lax.broadcasted_iota(I32, x.shape, 0)
    d = 1
    while d < c:
        x = x + jnp.where(row >= d, pltpu.roll(x, d, 0), 0.0)
        d *= 2
    return x


def _head_select(cols, shift, width):
    c = cols[0].shape[0]
    lane_head = lax.broadcasted_iota(I32, (c, width), 1) >> shift
    out = jnp.zeros((c, width), F32)
    for h, col in enumerate(cols):
        out = jnp.where(lane_head == h, col, out)
    return out


def _norm_matmul_kernel(x_ref, g_ref, *refs, n_w):
    w_refs, o_refs = refs[:n_w], refs[n_w:]
    xb = _rms(x_ref[...], g_ref[...]).astype(BF16)
    for w_ref, o_ref in zip(w_refs, o_refs):
        o_ref[...] = _dot(xb, w_ref[...]).astype(o_ref.dtype)


def _norm_matmul(x2d, gain, weights, out_dtypes, tm):
    t, d = x2d.shape
    n_w = len(weights)
    in_specs = [pl.BlockSpec((tm, d), lambda i: (i, 0)), pl.BlockSpec((1, d), lambda i: (0, 0))]
    in_specs += [pl.BlockSpec(w.shape, lambda i: (0, 0)) for w in weights]
    out_specs = [pl.BlockSpec((tm, w.shape[1]), lambda i: (i, 0)) for w in weights]
    out_shape = [jax.ShapeDtypeStruct((t, w.shape[1]), dt) for w, dt in zip(weights, out_dtypes)]
    return pl.pallas_call(
        functools.partial(_norm_matmul_kernel, n_w=n_w),
        grid=(t // tm,), in_specs=in_specs, out_specs=out_specs, out_shape=out_shape,
        compiler_params=_cparams("parallel"), name="norm_matmul",
    )(x2d, gain.reshape(1, d), *weights)


def _ret_kernel(blk_ref, cos_ref, sin_ref, o_ref, state_ref, *, chunk):
    c = chunk

    @pl.when(pl.program_id(1) == 0)
    def _():
        state_ref[...] = jnp.zeros_like(state_ref)

    cos, sin = cos_ref[...], sin_ref[...]
    q = blk_ref[:, 0:128]
    k = blk_ref[:, 128:256]
    v = blk_ref[:, 256:512]
    g = blk_ref[:, 512:768]
    q = q * cos + _rot_half(q, RET_QK_DIM // 2) * sin
    k = (k * cos + _rot_half(k, RET_QK_DIM // 2) * sin) * (RET_QK_DIM ** -0.5)

    log_gamma = [math.log(1.0 - 2.0 ** (-5.0 - h)) for h in range(N_HEADS)]
    lane_k = lax.broadcasted_iota(I32, (1, 128), 1) >> 5
    lg_lane = jnp.zeros((1, 128), F32)
    for h in range(N_HEADS):
        lg_lane = jnp.where(lane_k == h, log_gamma[h], lg_lane)
    t_col = lax.broadcasted_iota(I32, (c, 1), 0).astype(F32)
    qd = q * jnp.exp((t_col + 1.0) * lg_lane)
    kd = k * jnp.exp((float(c - 1) - t_col) * lg_lane)

    ti = lax.broadcasted_iota(I32, (c, c), 0)
    si = lax.broadcasted_iota(I32, (c, c), 1)
    causal = ti >= si
    diff = jnp.where(causal, ti - si, 0).astype(F32)
    lane_qk = lax.broadcasted_iota(I32, (c, 128), 1) >> 5
    lane_v = lax.broadcasted_iota(I32, (c, 256), 1) >> 6
    kb = k.astype(BF16)
    vb = v.astype(BF16)
    y = _dot(qd.astype(BF16), state_ref[...].astype(BF16))
    for h in range(N_HEADS):
        qm = jnp.where(lane_qk == h, q, 0.0).astype(BF16)
        s = _dot_nt(qm, kb)
        dec = jnp.where(causal, jnp.exp(diff * log_gamma[h]), 0.0)
        yh = _dot((s * dec).astype(BF16), vb)
        y = y + jnp.where(lane_v == h, yh, 0.0)

    row_h = lax.broadcasted_iota(I32, (128, 256), 0) >> 5
    col_h = lax.broadcasted_iota(I32, (128, 256), 1) >> 6
    lg_row = jnp.zeros((128, 256), F32)
    for h in range(N_HEADS):
        lg_row = jnp.where(row_h == h, log_gamma[h], lg_row)
    new = _dot_tn(kd.astype(BF16), vb)
    state_ref[...] = state_ref[...] * jnp.exp(float(c) * lg_row) + jnp.where(row_h == col_h, new, 0.0)

    ms = _dot_exact_rhs(y * y, _block_diag_ones(256, 6)) * (1.0 / HEAD_DIM)
    o_ref[...] = y * lax.rsqrt(ms + NORM_EPS) * _silu(g)


def _retention(ret_p, cos32, sin32):
    b, s, _ = ret_p.shape
    c = min(RET_CHUNK, s)
    return pl.pallas_call(
        functools.partial(_ret_kernel, chunk=c),
        grid=(b, s // c),
        in_specs=[pl.BlockSpec((None, c, 768), lambda i, j: (i, j, 0)),
                  pl.BlockSpec((c, 128), lambda i, j: (j, 0)),
                  pl.BlockSpec((c, 128), lambda i, j: (j, 0))],
        out_specs=pl.BlockSpec((None, c, 256), lambda i, j: (i, j, 0)),
        out_shape=jax.ShapeDtypeStruct((b, s, 256), F32),
        scratch_shapes=[pltpu.VMEM((128, 256), F32)],
        compiler_params=_cparams("parallel", "arbitrary"), name="retention",
    )(ret_p, cos32, sin32)


def _ssd_kernel(blk_ref, cw_ref, cb_ref, dtb_ref, aneg_ref, dsk_ref, nw_ref, o_ref,
                state_ref, tail_ref, *, chunk):
    c = chunk

    @pl.when(pl.program_id(1) == 0)
    def _():
        state_ref[...] = jnp.zeros_like(state_ref)
        tail_ref[...] = jnp.zeros_like(tail_ref)

    z = blk_ref[:, 0:256]
    xbc = blk_ref[:, 256:1024]
    dt_raw = blk_ref[:, 1024:1152]

    tail = tail_ref[...]
    conv = xbc * cw_ref[SSM_CONV - 1:SSM_CONV, :] + cb_ref[...]
    for m in range(1, SSM_CONV):
        conv = conv + _shift_rows(tail, xbc, m) * cw_ref[SSM_CONV - 1 - m:SSM_CONV - m, :]
    tail_ref[...] = xbc[c - SUBLANES:c, :]
    xbc = _silu(conv)
    xs = xbc[:, 0:256]
    bm = xbc[:, 256:512]
    cm = xbc[:, 512:768]

    u = dt_raw + dtb_ref[...]
    dt = jnp.maximum(u, 0.0) + jnp.log1p(jnp.exp(-jnp.abs(u)))
    la = dt * aneg_ref[...]
    cum = _cumsum_rows(la)
    cum_t = cum.T
    cum_last = cum[c - 1:c, :]

    dt_lane = _head_select([dt[:, h:h + 1] for h in range(N_HEADS)], 6, 256)
    cum_lane = _head_select([cum[:, h:h + 1] for h in range(N_HEADS)], 6, 256)
    end_lane = _head_select([jnp.broadcast_to(cum_last[:, h:h + 1], (c, 1)) for h in range(N_HEADS)], 6, 256)
    xdt = xs * dt_lane
    x_end = (xdt * jnp.exp(end_lane - cum_lane)).astype(BF16)
    xdt_b = xdt.astype(BF16)
    e_lane = jnp.exp(cum_lane)
    chunk_decay = jnp.exp(end_lane[0:1, :])

    ti = lax.broadcasted_iota(I32, (c, c), 0)
    si = lax.broadcasted_iota(I32, (c, c), 1)
    causal = ti >= si
    lane_half = lax.broadcasted_iota(I32, (c, 128), 1) >> 6
    ys = []
    for grp in range(2):
        sl = slice(128 * grp, 128 * grp + 128)
        cg = cm[:, sl].astype(BF16)
        bg = bm[:, sl].astype(BF16)
        s = _dot_nt(cg, bg)
        parts = []
        for hh in range(2):
            h = 2 * grp + hh
            seg = jnp.minimum(cum[:, h:h + 1] - cum_t[h:h + 1, :], 0.0)
            dec = jnp.where(causal, jnp.exp(seg), 0.0)
            parts.append(_dot((s * dec).astype(BF16), xdt_b[:, sl]))
        y_g = jnp.where(lane_half == 0, parts[0], parts[1])
        st = state_ref[grp]
        y_g = y_g + e_lane[:, sl] * _dot(cg, st.astype(BF16))
        state_ref[grp] = st * chunk_decay[:, sl] + _dot_tn(bg, x_end[:, sl])
        ys.append(y_g)
    y = jnp.concatenate(ys, axis=1)
    y = (y + xs * dsk_ref[...]) * _silu(z)
    o_ref[...] = _rms(y, nw_ref[...])


def _ssd(ssm_p, conv_w, conv_b, dt_bias, a_log, d_skip, norm_w):
    b, s, _ = ssm_p.shape
    c = min(SSD_CHUNK, s)
    pad4 = lambda v: jnp.pad(v.astype(F32), (0, 128 - N_HEADS)).reshape(1, 128)
    params = [conv_w.astype(F32), conv_b.reshape(1, SSM_XBC), pad4(dt_bias),
              pad4(-jnp.exp(a_log.astype(F32))),
              jnp.repeat(d_skip, HEAD_DIM).reshape(1, 256), norm_w.reshape(1, 256)]
    full = lambda a: pl.BlockSpec(a.shape, lambda i, j: (0,) * a.ndim)
    return pl.pallas_call(
        functools.partial(_ssd_kernel, chunk=c),
        grid=(b, s // c),
        in_specs=[pl.BlockSpec((None, c, 1152), lambda i, j: (i, j, 0))] + [full(p) for p in params],
        out_specs=pl.BlockSpec((None, c, 256), lambda i, j: (i, j, 0)),
        out_shape=jax.ShapeDtypeStruct((b, s, 256), F32),
        scratch_shapes=[pltpu.VMEM((2, 128, 128), F32), pltpu.VMEM((SUBLANES, SSM_XBC), F32)],
        compiler_params=_cparams("parallel", "arbitrary"), name="ssd",
    )(ssm_p, *params)


def _rwkv_chunk(f, tail, m0, prm, c):
    mu, w0, w2, a0, a2, g2, k_k, k_a, r_k, ln_w, ln_b = prm
    n = N_HEADS * c
    prev = _shift_rows(tail, f, 1)
    x = f + (prev - f) * mu
    r = x[:, 0:256]
    k = x[:, 256:512]
    v = x[:, 512:768]
    lora = x[:, 768:896]
    gl = x[:, 896:1024]

    logw = -RW_DECAY_SCALE * jax.nn.sigmoid(w0 + _dot3(jnp.tanh(lora), w2))
    a = jax.nn.sigmoid(a0 + _dot3(lora, a2))
    gate = _dot(jax.nn.sigmoid(gl).astype(BF16), g2)
    bd = _block_diag_ones(256, 6)
    kk = k * k_k
    kk = kk * lax.rsqrt(_dot_exact_rhs(kk * kk, bd) + 1e-12)
    k2 = k * (1.0 + (a - 1.0) * k_a)

    cum = _cumsum_rows(logw)
    cum_end = cum[c - 1:c, :]
    g_inc = jnp.exp(cum)
    g_exc = jnp.exp(cum - logw)
    g_inv = jnp.exp(-cum)
    g_end = jnp.exp(cum_end - cum)

    lane_h = lax.broadcasted_iota(I32, (c, 256), 1) >> 6

    def stack(t):
        return jnp.concatenate([jnp.where(lane_h == h, t, 0.0) for h in range(N_HEADS)], axis=0).astype(BF16)

    a_s = stack(-kk * g_exc)
    b_s = stack(kk * a * g_inv)
    k_s = stack(k2 * g_inv)
    r_s = stack(r * g_inc)
    v_s = stack(v)
    b_end = stack(kk * a * g_end)
    k_end = stack(k2 * g_end)

    row = lax.broadcasted_iota(I32, (n, n), 0)
    col = lax.broadcasted_iota(I32, (n, n), 1)
    strict = row > col
    incl = row >= col
    l_ab = jnp.where(strict, _dot_nt(a_s, b_s), 0.0)
    l_ak = jnp.where(strict, _dot_nt(a_s, k_s), 0.0).astype(BF16)
    p_rb = jnp.where(incl, _dot_nt(r_s, b_s), 0.0).astype(BF16)
    p_rk = jnp.where(incl, _dot_nt(r_s, k_s), 0.0).astype(BF16)

    m0b = m0.astype(BF16)
    u = _dot(a_s, m0b) + _dot(l_ak, v_s)
    p = l_ab.astype(BF16)
    steps = max(1, int(math.ceil(math.log2(c))))
    for i in range(steps):
        u = u + _dot(p, u.astype(BF16))
        if i + 1 < steps:
            p = _dot(p, p).astype(BF16)
    ub = u.astype(BF16)
    y_s = _dot(r_s, m0b) + _dot(p_rb, ub) + _dot(p_rk, v_s)
    y = y_s[0:c]
    for h in range(1, N_HEADS):
        y = y + y_s[h * c:(h + 1) * c]

    ones = jnp.ones((c, 128), BF16)
    lw_hi, lw_lo = _split(logw)
    lw_lo2 = (logw - lw_hi.astype(F32) - lw_lo.astype(F32)).astype(BF16)
    tot = _dot_tn(lw_hi, ones) + _dot_tn(lw_lo, ones) + _dot_tn(lw_lo2, ones)
    g_col = jnp.exp(jnp.concatenate([tot, tot], axis=1))
    m_new = m0 * g_col + _dot_tn(b_end, ub) + _dot_tn(k_end, v_s)

    inv_n = 1.0 / HEAD_DIM
    mean = _dot_exact_rhs(y, bd) * inv_n
    yc = y - mean
    var = _dot_exact_rhs(yc * yc, bd) * inv_n
    yn = yc * lax.rsqrt(var + RW_GN_EPS) * ln_w + ln_b
    bonus = _dot_exact_rhs(r * k2 * r_k, bd) * v
    return (yn + bonus) * gate, m_new


def _rwkv_kernel(blk_ref, mu_ref, w0_ref, w2_ref, a0_ref, a2_ref, g2_ref, kk_ref, ka_ref, rk_ref,
                 lnw_ref, lnb_ref, o_ref, state_ref, prev_ref, *, chunk, n_seq):
    c = chunk

    @pl.when(pl.program_id(1) == 0)
    def _():
        state_ref[...] = jnp.zeros_like(state_ref)
        prev_ref[...] = jnp.zeros_like(prev_ref)

    prm = (mu_ref[...], w0_ref[...], w2_ref[...], a0_ref[...], a2_ref[...], g2_ref[...].astype(BF16),
           kk_ref[...], ka_ref[...], rk_ref[...], lnw_ref[...], lnb_ref[...])
    for i in range(n_seq):
        f = blk_ref[i]
        out, m_new = _rwkv_chunk(f, prev_ref[i], state_ref[i], prm, c)
        prev_ref[i] = f[c - SUBLANES:c, :]
        state_ref[i] = m_new
        o_ref[i] = out


def _rwkv(rw_p, mu, w0, w2, a0, a2, g2, k_k, k_a, r_k, ln_w, ln_b):
    b, s, _ = rw_p.shape
    c = min(RW_CHUNK, s)
    row = lambda v: v.astype(F32).reshape(1, -1)
    w2p = jnp.concatenate([w2, jnp.zeros_like(w2)], axis=0)
    a2p = jnp.concatenate([jnp.zeros_like(a2), a2], axis=0)
    params = [row(mu), row(w0), w2p, row(a0), a2p, g2, row(k_k), row(k_a), row(r_k), row(ln_w), row(ln_b)]
    full = lambda a: pl.BlockSpec(a.shape, lambda i, j: (0,) * a.ndim)
    n_seq = RW_SEQS_PER_STEP if b % RW_SEQS_PER_STEP == 0 else 1
    return pl.pallas_call(
        functools.partial(_rwkv_kernel, chunk=c, n_seq=n_seq),
        grid=(b // n_seq, s // c),
        in_specs=[pl.BlockSpec((n_seq, c, 1024), lambda i, j: (i, j, 0))] + [full(p) for p in params],
        out_specs=pl.BlockSpec((n_seq, c, 256), lambda i, j: (i, j, 0)),
        out_shape=jax.ShapeDtypeStruct((b, s, 256), F32),
        scratch_shapes=[pltpu.VMEM((n_seq, 256, 256), F32), pltpu.VMEM((n_seq, SUBLANES, 1024), F32)],
        compiler_params=_cparams("parallel", "arbitrary"), name="rwkv7",
    )(rw_p, *params)


def _dsa_kernel(q_ref, iq_ref, kv_ref, iki_ref, cos32_ref, sin32_ref, cos64_ref, sin64_ref, g_ref,
                o_ref, ik3_ref, kx_ref, vx_ref, keys_ref, hi_ref, lo_ref, m_ref, acc_ref, s_ref, mn_ref, p_ref,
                *, seq, topk, kc):
    qb = DSA_QBLOCK
    j = pl.program_id(1)
    n_kc = (j * qb + qb + kc - 1) // kc

    @pl.when(j == 0)
    def _():
        lane = lax.broadcasted_iota(I32, (seq, 128), 1)
        kv = kv_ref[...]
        kr = kv * cos64_ref[:, 0:128] + _rot_half(kv, HEAD_DIM // 2) * sin64_ref[:, 0:128]
        kx_ref[...] = jnp.where(lane < HEAD_DIM, kr, 0.0).astype(BF16)
        vx_ref[...] = jnp.where(lane < HEAD_DIM, pltpu.roll(kv, 64, 1), 1.0).astype(BF16)
        ik = jnp.where(lane < IDX_DIM, iki_ref[...], 0.0)
        ik = ik * lax.rsqrt(jnp.sum(ik * ik, axis=-1, keepdims=True) * (1.0 / IDX_DIM) + NORM_EPS) * g_ref[...]
        ik = ik * cos32_ref[...] + _rot_half(ik, IDX_DIM // 2) * sin32_ref[...]
        ik = jnp.where(lane < IDX_DIM, ik, 0.0)
        hi, lo = _split(ik)
        hi, lo = hi.astype(F32), lo.astype(F32)
        ik3_ref[...] = (hi + pltpu.roll(hi, 32, 1) + pltpu.roll(lo, 64, 1)).astype(BF16)

    q0 = pl.multiple_of(j * qb, qb)
    q_pos = q0 + lax.broadcasted_iota(I32, (qb, 1), 0)
    lane = lax.broadcasted_iota(I32, (qb, 128), 1)

    iq = iq_ref[...]
    iq = iq * cos32_ref[pl.ds(q0, qb), :] + _rot_half(iq, IDX_DIM // 2) * sin32_ref[pl.ds(q0, qb), :]
    iq_hi, iq_lo = _split(iq)
    iq_hi, iq_lo = iq_hi.astype(F32), iq_lo.astype(F32)
    iq_lhs = []
    for h in range(IDX_HEADS):
        mine = (lane >> 5) == h
        a0 = jnp.where(mine, iq_hi, 0.0)
        b0 = jnp.where(mine, iq_lo, 0.0)
        if h:
            a0 = pltpu.roll(a0, 128 - 32 * h, 1)
            b0 = pltpu.roll(b0, 128 - 32 * h, 1)
        iq_lhs.append((a0 + pltpu.roll(b0, 32, 1) + pltpu.roll(a0, 64, 1)).astype(BF16))
    iw = iki_ref[pl.ds(q0, qb), :] * (IDX_HEADS ** -0.5 * IDX_DIM ** -0.5)
    iw_cols = [iw[:, IDX_DIM + h:IDX_DIM + h + 1] for h in range(IDX_HEADS)]
    col_iota = lax.broadcasted_iota(I32, (qb, kc), 1)

    def score_body(ci, carry):
        off = pl.multiple_of(ci * kc, kc)
        ikc = ik3_ref[pl.ds(off, kc), :]
        sc = jnp.maximum(_dot_nt(iq_lhs[0], ikc), 0.0) * iw_cols[0]
        for h in range(1, IDX_HEADS):
            sc = sc + jnp.maximum(_dot_nt(iq_lhs[h], ikc), 0.0) * iw_cols[h]
        sc = sc + 0.0
        sc = jnp.where(off + col_iota <= q_pos, sc, -jnp.inf)
        bits = lax.bitcast_convert_type(sc, I32)
        key = bits ^ ((bits >> 31) & 0x7FFFFFFF)
        keys_ref[:, pl.ds(off, kc)] = key
        hi_ref[:, pl.ds(off, kc)] = (key >> 16).astype(I16)
        return carry

    lax.fori_loop(0, n_kc, score_body, 0)

    def count(ref, pred, one, zero):
        def body(ci, acc):
            off = pl.multiple_of(ci * kc, kc)
            x = jnp.where(pred(ref[:, pl.ds(off, kc)], off), one, zero)
            part = x[:, 0:128]
            for t in range(1, kc // 128):
                part = part + x[:, t * 128:(t + 1) * 128]
            return acc + part
        acc = lax.fori_loop(0, n_kc, body, jnp.zeros((qb, 128), one.dtype))
        return jnp.sum(acc.astype(F32), axis=1, keepdims=True)

    def count16(ref, pred):
        return count(ref, pred, jnp.int16(1), jnp.int16(0))

    def count32(pred):
        return count(keys_ref, pred, jnp.float32(1), jnp.float32(0))

    def search16(ref, want):
        def bit_body(i, t_u):
            cand_u = t_u | lax.shift_left(jnp.int32(1), jnp.int32(15) - i)
            cand = (cand_u - 32768).astype(I16)
            cnt = count16(ref, lambda kk_, off: kk_ >= cand)
            return jnp.where(cnt >= want, cand_u, t_u)
        return lax.fori_loop(0, 16, bit_body, jnp.zeros((qb, 1), I32)) - 32768

    t_hi = search16(hi_ref, float(topk))
    t_hi16 = t_hi.astype(I16)
    want_lo = float(topk) - count16(hi_ref, lambda kk_, off: kk_ > t_hi16)

    def lo_body(ci, carry):
        off = pl.multiple_of(ci * kc, kc)
        low = ((keys_ref[:, pl.ds(off, kc)] & 0xFFFF) - 32768).astype(I16)
        lo_ref[:, pl.ds(off, kc)] = jnp.where(hi_ref[:, pl.ds(off, kc)] == t_hi16, low, jnp.int16(-32768))
        return carry

    lax.fori_loop(0, n_kc, lo_body, 0)
    t_lo = search16(lo_ref, want_lo)
    tau = lax.shift_left(t_hi, 16) | (t_lo + 32768)
    need = float(topk) - count32(lambda kk_, off: kk_ > tau)
    ties = count32(lambda kk_, off: kk_ == tau)

    idx_bits = max(1, int(math.ceil(math.log2(seq))))

    def idx_search():
        def idx_body(i, cst):
            cand = cst | lax.shift_left(jnp.int32(1), jnp.int32(idx_bits - 1) - i)
            cnt = count32(lambda kk_, off: (kk_ == tau) & (off + col_iota < cand))
            return jnp.where(cnt < need, cand, cst)
        return lax.fori_loop(0, idx_bits, idx_body, jnp.zeros((qb, 1), I32))

    c_star = lax.cond(jnp.max(ties - need) > 0.5, idx_search, lambda: jnp.full((qb, 1), seq, I32))

    q = q_ref[...]
    q = (q * cos64_ref[pl.ds(q0, qb), :] + _rot_half(q, HEAD_DIM // 2) * sin64_ref[pl.ds(q0, qb), :]) * (HEAD_DIM ** -0.5)
    q_lhs = []
    for h in range(N_HEADS):
        half = q[:, 128 * (h >> 1):128 * (h >> 1) + 128]
        half = jnp.where((lane >> 6) == (h & 1), half, 0.0)
        q_lhs.append((pltpu.roll(half, 64, 1) if h & 1 else half).astype(BF16))
    neg = -1e30
    m_ref[...] = jnp.full_like(m_ref, neg)
    acc_ref[...] = jnp.zeros_like(acc_ref)

    def attn_body(ci, carry):
        off = pl.multiple_of(ci * kc, kc)
        kk_ = keys_ref[:, pl.ds(off, kc)]
        colp = off + col_iota
        sel = ((kk_ > tau) | ((kk_ == tau) & (colp <= c_star))) & (colp <= q_pos)
        bias = jnp.where(sel, 0.0, neg)
        kx = kx_ref[pl.ds(off, kc), :]
        vx = vx_ref[pl.ds(off, kc), :]
        for h in range(N_HEADS):
            s_ref[h] = _dot_nt(q_lhs[h], kx) + bias
        for h in range(N_HEADS):
            mn_ref[h] = jnp.maximum(m_ref[h], jnp.max(s_ref[h], axis=-1, keepdims=True))
        for h in range(N_HEADS):
            p_ref[h] = jnp.exp(s_ref[h] - mn_ref[h]).astype(BF16)
        for h in range(N_HEADS):
            acc_ref[h] = jnp.exp(m_ref[h] - mn_ref[h]) * acc_ref[h] + _dot(p_ref[h], vx)
            m_ref[h] = mn_ref[h]
        return carry

    lax.fori_loop(0, n_kc, attn_body, 0)
    halves = []
    for pair in range(N_HEADS // 2):
        outs = []
        for h in (2 * pair, 2 * pair + 1):
            a = acc_ref[h]
            outs.append(a / pltpu.roll(a, 64, 1))
        halves.append(jnp.where(lane < HEAD_DIM, outs[0], pltpu.roll(outs[1], 64, 1)))
    o_ref[...] = jnp.concatenate(halves, axis=1)


def _dsa(dsa_p, idx_k_norm, cos32, sin32, cos64, sin64):
    b, s, _ = dsa_p.shape
    qb = DSA_QBLOCK
    kc = min(DSA_KEY_CHUNK, s)
    topk = min(DSA_TOPK_MAX, s // 4)
    gpad = jnp.pad(idx_k_norm.astype(F32), (0, 128 - IDX_DIM)).reshape(1, 128)
    full = lambda a: pl.BlockSpec(a.shape, lambda i, j: (0,) * a.ndim)
    return pl.pallas_call(
        functools.partial(_dsa_kernel, seq=s, topk=topk, kc=kc),
        grid=(b, s // qb),
        in_specs=[pl.BlockSpec((None, qb, 256), lambda i, j: (i, j, 0)),
                  pl.BlockSpec((None, qb, 128), lambda i, j: (i, j, 2)),
                  pl.BlockSpec((None, s, 128), lambda i, j: (i, 0, 3)),
                  pl.BlockSpec((None, s, 128), lambda i, j: (i, 0, 4)),
                  full(cos32), full(sin32), full(cos64), full(sin64), full(gpad)],
        out_specs=pl.BlockSpec((None, qb, 256), lambda i, j: (i, j, 0)),
        out_shape=jax.ShapeDtypeStruct((b, s, 256), F32),
        scratch_shapes=[pltpu.VMEM((s, 128), BF16), pltpu.VMEM((s, 128), BF16), pltpu.VMEM((s, 128), BF16),
                        pltpu.VMEM((qb, s), I32), pltpu.VMEM((qb, s), I16), pltpu.VMEM((qb, s), I16),
                        pltpu.VMEM((N_HEADS, qb, 1), F32), pltpu.VMEM((N_HEADS, qb, 128), F32),
                        pltpu.VMEM((N_HEADS, qb, kc), F32), pltpu.VMEM((N_HEADS, qb, 1), F32),
                        pltpu.VMEM((N_HEADS, qb, kc), BF16)],
        compiler_params=_cparams("parallel", "arbitrary"), name="dsa",
    )(dsa_p, dsa_p, dsa_p, dsa_p, cos32, sin32, cos64, sin64, gpad)


def _mix_cross_kernel(h_ref, o0_ref, o1_ref, o2_ref, o3_ref, wout_ref, g_ref, wq_ref, k_ref, v_ref, wo_ref,
                      out_ref):
    h = h_ref[...]
    for i, o_ref in enumerate((o0_ref, o1_ref, o2_ref, o3_ref)):
        h = h + _dot(o_ref[...].astype(BF16), wout_ref[i * GROUP_WIDTH:(i + 1) * GROUP_WIDTH, :])
    q = _dot(_rms(h, g_ref[...]).astype(BF16), wq_ref[...]) * (X_HEAD_DIM ** -0.5)
    outs = []
    for hd in range(X_HEADS):
        sl = slice(hd * X_HEAD_DIM, (hd + 1) * X_HEAD_DIM)
        s = _dot_nt(q[:, sl].astype(BF16), k_ref[:, sl])
        p = jnp.exp(s - jnp.max(s, axis=-1, keepdims=True))
        p = p / jnp.sum(p, axis=-1, keepdims=True)
        outs.append(_dot(p.astype(BF16), v_ref[:, sl]))
    o = jnp.concatenate(outs, axis=1).astype(BF16)
    out_ref[...] = h + _dot(o, wo_ref[...])


def _mix_cross(h2d, outs, w_out, g_cross, wq, k_mem, v_mem, wo, seq, tm):
    t, d = h2d.shape
    n_mem = k_mem.shape[1]
    per_seq = seq // tm
    tok = lambda w: pl.BlockSpec((tm, w), lambda i: (i, 0))
    const = lambda a: pl.BlockSpec(a.shape, lambda i: (0,) * a.ndim)
    mem = pl.BlockSpec((None, n_mem, d), lambda i: (i // per_seq, 0, 0))
    return pl.pallas_call(
        _mix_cross_kernel,
        grid=(t // tm,),
        in_specs=[tok(d)] + [tok(GROUP_WIDTH)] * 4 + [const(w_out), pl.BlockSpec((1, d), lambda i: (0, 0)),
                                                     const(wq), mem, mem, const(wo)],
        out_specs=tok(d),
        out_shape=jax.ShapeDtypeStruct((t, d), F32),
        compiler_params=_cparams("parallel"), name="mix_cross",
    )(h2d, *outs, w_out, g_cross.reshape(1, d), wq, k_mem, v_mem, wo)


def _ffn_kernel(h_ref, g_ref, wg_ref, wv_ref, cw_ref, cb_ref, wd_ref, gf_ref, out_ref, tail_ref, acc_ref,
                *, per_seq, n_chunks, final_norm):
    tm = h_ref.shape[0]

    @pl.when(pl.program_id(0) % per_seq == 0)
    def _():
        tail_ref[...] = jnp.zeros_like(tail_ref)

    h = h_ref[...]
    xb = _rms(h, g_ref[...]).astype(BF16)
    acc_ref[...] = jnp.zeros_like(acc_ref)

    def body(ci, carry):
        gate = _dot(xb, wg_ref[ci])
        val = _dot(xb, wv_ref[ci])
        tail = tail_ref[ci]
        cw = cw_ref[ci]
        conv = gate * cw[FFN_CONV - 1:FFN_CONV, :] + cb_ref[ci]
        for m in range(1, FFN_CONV):
            conv = conv + _shift_rows(tail, gate, m) * cw[FFN_CONV - 1 - m:FFN_CONV - m, :]
        tail_ref[ci] = gate[tm - SUBLANES:tm, :]
        acc_ref[...] += _dot((_silu(conv) * val).astype(BF16), wd_ref[ci])
        return carry

    lax.fori_loop(0, n_chunks, body, 0, unroll=True)
    out = h + acc_ref[...]
    if final_norm:
        out = _rms(out, gf_ref[...])
    out_ref[...] = out


def _ffn(h2d, g_ffn, w_up, conv_w, conv_b, w_down, g_final, seq, tm, final_norm):
    t, d = h2d.shape
    fc = FF_CHUNK
    nck = D_FF // fc
    chunked = lambda w: w.reshape(w.shape[0], nck, fc).transpose(1, 0, 2)
    wg = chunked(w_up[:, :D_FF]).astype(BF16)
    wv = chunked(w_up[:, D_FF:]).astype(BF16)
    cw = chunked(conv_w.astype(F32))
    cb = conv_b.astype(F32).reshape(nck, 1, fc)
    wd = w_down.reshape(nck, fc, d).astype(BF16)
    const = lambda a: pl.BlockSpec(a.shape, lambda i: (0,) * a.ndim)
    row = pl.BlockSpec((1, d), lambda i: (0, 0))
    return pl.pallas_call(
        functools.partial(_ffn_kernel, per_seq=seq // tm, n_chunks=nck, final_norm=final_norm),
        grid=(t // tm,),
        in_specs=[pl.BlockSpec((tm, d), lambda i: (i, 0)), row, const(wg), const(wv), const(cw), const(cb),
                  const(wd), row],
        out_specs=pl.BlockSpec((tm, d), lambda i: (i, 0)),
        out_shape=jax.ShapeDtypeStruct((t, d), F32),
        scratch_shapes=[pltpu.VMEM((nck, SUBLANES, fc), F32), pltpu.VMEM((tm, d), F32)],
        compiler_params=_cparams("arbitrary"), name="conv_glu",
    )(h2d, g_ffn.reshape(1, d), wg, wv, cw, cb, wd, g_final.reshape(1, d))


def _rope_tables(seq, head_dim, width):
    half = head_dim // 2
    inv = ROPE_THETA ** (-jnp.arange(half, dtype=F32) / half)
    ang = jnp.arange(seq, dtype=F32)[:, None] * inv[None, :]
    reps = width // half
    return jnp.tile(jnp.cos(ang), (1, reps)), jnp.tile(jnp.sin(ang), (1, reps))


def _split_w_in(w):
    ret = w[:, 0:768]
    rw = w[:, 768:1792]
    ssm = jnp.pad(w[:, 1792:2820], ((0, 0), (0, 1152 - 1028)))
    d = w[:, 2820:3368]
    dsa = jnp.concatenate([d[:, 0:256], d[:, 384:512], d[:, 256:384], d[:, 512:548],
                           jnp.zeros((w.shape[0], 128 - 36), w.dtype)], axis=1)
    return [t.astype(BF16) for t in (ret, rw, ssm, dsa)]


def kernel(x, mem, norm_mix, w_in, rwkv_mu, rwkv_w0, rwkv_w2, rwkv_a0, rwkv_a2, rwkv_g2, rwkv_k_k, rwkv_k_a, rwkv_r_k, rwkv_ln_w, rwkv_ln_b, ssm_conv_w, ssm_conv_b, ssm_dt_bias, ssm_a_log, ssm_d, ssm_norm, idx_k_norm, w_out, norm_cross, norm_mem, wq_x, wk_x, wv_x, wo_x, norm_ffn, w_up, ffn_conv_w, ffn_conv_b, w_down, norm_final):
    b, s, d = x.shape
    n_mem = mem.shape[1]
    depth = w_in.shape[0]
    t = b * s
    tm = min(512, s)
    cos32, sin32 = _rope_tables(s, RET_QK_DIM, 128)
    cos64, sin64 = _rope_tables(s, HEAD_DIM, 256)
    mem2d = mem.reshape(b * n_mem, d)
    h = x.reshape(t, d)
    for l in range(depth):
        ret_p, rw_p, ssm_p, dsa_p = _norm_matmul(h, norm_mix[l], _split_w_in(w_in[l]), [F32] * 4, tm)
        o_ret = _retention(ret_p.reshape(b, s, -1), cos32, sin32)
        o_rw = _rwkv(rw_p.reshape(b, s, -1), rwkv_mu[l], rwkv_w0[l], rwkv_w2[l], rwkv_a0[l], rwkv_a2[l],
                     rwkv_g2[l], rwkv_k_k[l], rwkv_k_a[l], rwkv_r_k[l], rwkv_ln_w[l], rwkv_ln_b[l])
        o_ssm = _ssd(ssm_p.reshape(b, s, -1), ssm_conv_w[l], ssm_conv_b[l], ssm_dt_bias[l], ssm_a_log[l],
                     ssm_d[l], ssm_norm[l])
        o_dsa = _dsa(dsa_p.reshape(b, s, -1), idx_k_norm[l], cos32, sin32, cos64, sin64)
        k_mem, v_mem = _norm_matmul(mem2d, norm_mem[l], [wk_x[l].astype(BF16), wv_x[l].astype(BF16)],
                                    [BF16, BF16], min(512, b * n_mem))
        outs = [o.reshape(t, GROUP_WIDTH) for o in (o_ret, o_rw, o_ssm, o_dsa)]
        h = _mix_cross(h, outs, w_out[l].astype(BF16), norm_cross[l], wq_x[l].astype(BF16),
                       k_mem.reshape(b, n_mem, d), v_mem.reshape(b, n_mem, d), wo_x[l].astype(BF16), s, tm)
        h = _ffn(h, norm_ffn[l], w_up[l], ffn_conv_w[l], ffn_conv_b[l], w_down[l], norm_final, s, tm,
                 final_norm=(l == depth - 1))
    return h.reshape(b, s, d)
```

```python
import functools
import math

import jax
import jax.numpy as jnp
import numpy as np
from jax import lax
from jax.experimental import pallas as pl
from jax.experimental.pallas import tpu as pltpu

F32 = jnp.float32
BF16 = jnp.bfloat16
I32 = jnp.int32
I16 = jnp.int16

D_MODEL = 1024
GROUP_WIDTH = 256
HEAD_DIM = 64
N_HEADS = 4
RET_QK_DIM = 32
RW_DECAY_SCALE = math.exp(-0.5)
RW_GN_EPS = 64e-5
SSM_STATE = 128
SSM_CONV = 4
SSM_XBC = 768
IDX_HEADS = 4
IDX_DIM = 32
DSA_TOPK_MAX = 256
DSA_QBLOCK = 128
X_HEADS = 4
X_HEAD_DIM = 256
D_FF = 2816
FFN_CONV = 3
ROPE_THETA = 10000.0
NORM_EPS = 1e-6

VMEM_LIMIT_BYTES = 52 * 1024 * 1024
SUBLANES = 8
LANES = 128

RET_CHUNK = 256
SSD_CHUNK = 128
RW_CHUNK = 64
RW_SEQS_PER_STEP = 2
DSA_KEY_CHUNK = 512
FF_CHUNK = 256


def _cparams(*sem):
    return pltpu.CompilerParams(dimension_semantics=sem, vmem_limit_bytes=VMEM_LIMIT_BYTES)


def _dot(a, b):
    return jnp.dot(a, b, preferred_element_type=F32)


def _dot_nt(a, b):
    return lax.dot_general(a, b, (((1,), (1,)), ((), ())), preferred_element_type=F32)


def _dot_tn(a, b):
    return lax.dot_general(a, b, (((0,), (0,)), ((), ())), preferred_element_type=F32)


def _split(a):
    hi = a.astype(BF16)
    lo = (a - hi.astype(F32)).astype(BF16)
    return hi, lo


def _dot_exact_rhs(a, b_bf16):
    hi, lo = _split(a)
    return _dot(hi, b_bf16) + _dot(lo, b_bf16)


def _dot3(a, b):
    ah, al = _split(a)
    bh, bl = _split(b)
    return _dot(ah, bh) + (_dot(ah, bl) + _dot(al, bh))


def _rms(x, g):
    return x * lax.rsqrt(jnp.mean(x * x, axis=-1, keepdims=True) + NORM_EPS) * g


def _silu(x):
    return x * jax.nn.sigmoid(x)


def _block_diag_ones(n, group_shift):
    r = lax.broadcasted_iota(I32, (n, n), 0) >> group_shift
    c = lax.broadcasted_iota(I32, (n, n), 1) >> group_shift
    return jnp.where(r == c, 1.0, 0.0).astype(BF16)


def _rot_half(x, half):
    n = x.shape[-1]
    lane = lax.broadcasted_iota(I32, x.shape, 1)
    first = (lane & (2 * half - 1)) < half
    return jnp.where(first, -pltpu.roll(x, n - half, 1), pltpu.roll(x, half, 1))


def _shift_rows(tail8, x, m):
    c = x.shape[0]
    xx = jnp.concatenate([tail8, x], axis=0)
    return pltpu.roll(xx, m, 0)[SUBLANES:SUBLANES + c]


def _cumsum_rows(x):
    c = x.shape[0]
    row = lax.broadcasted_iota(I32, x.shape, 0)
    d = 1
    while d < c:
        x = x + jnp.where(row >= d, pltpu.roll(x, d, 0), 0.0)
        d *= 2
    return x


def _head_select(cols, shift, width):
    c = cols[0].shape[0]
    lane_head = lax.broadcasted_iota(I32, (c, width), 1) >> shift
    out = jnp.zeros((c, width), F32)
    for h, col in enumerate(cols):
        out = jnp.where(lane_head == h, col, out)
    return out


def _norm_matmul_kernel(x_ref, g_ref, *refs, n_w):
    w_refs, o_refs = refs[:n_w], refs[n_w:]
    xb = _rms(x_ref[...], g_ref[...]).astype(BF16)
    for w_ref, o_ref in zip(w_refs, o_refs):
        o_ref[...] = _dot(xb, w_ref[...]).astype(o_ref.dtype)


def _norm_matmul(x2d, gain, weights, out_dtypes, tm):
    t, d = x2d.shape
    n_w = len(weights)
    in_specs = [pl.BlockSpec((tm, d), lambda i: (i, 0)), pl.BlockSpec((1, d), lambda i: (0, 0))]
    in_specs += [pl.BlockSpec(w.shape, lambda i: (0, 0)) for w in weights]
    out_specs = [pl.BlockSpec((tm, w.shape[1]), lambda i: (i, 0)) for w in weights]
    out_shape = [jax.ShapeDtypeStruct((t, w.shape[1]), dt) for w, dt in zip(weights, out_dtypes)]
    return pl.pallas_call(
        functools.partial(_norm_matmul_kernel, n_w=n_w),
        grid=(t // tm,), in_specs=in_specs, out_specs=out_specs, out_shape=out_shape,
        compiler_params=_cparams("parallel"), name="norm_matmul",
    )(x2d, gain.reshape(1, d), *weights)


def _ret_kernel(blk_ref, cos_ref, sin_ref, o_ref, state_ref, *, chunk):
    c = chunk

    @pl.when(pl.program_id(1) == 0)
    def _():
        state_ref[...] = jnp.zeros_like(state_ref)

    cos, sin = cos_ref[...], sin_ref[...]
    q = blk_ref[:, 0:128]
    k = blk_ref[:, 128:256]
    v = blk_ref[:, 256:512]
    g = blk_ref[:, 512:768]
    q = q * cos + _rot_half(q, RET_QK_DIM // 2) * sin
    k = (k * cos + _rot_half(k, RET_QK_DIM // 2) * sin) * (RET_QK_DIM ** -0.5)

    log_gamma = [math.log(1.0 - 2.0 ** (-5.0 - h)) for h in range(N_HEADS)]
    lane_k = lax.broadcasted_iota(I32, (1, 128), 1) >> 5
    lg_lane = jnp.zeros((1, 128), F32)
    for h in range(N_HEADS):
        lg_lane = jnp.where(lane_k == h, log_gamma[h], lg_lane)
    t_col = lax.broadcasted_iota(I32, (c, 1), 0).astype(F32)
    qd = q * jnp.exp((t_col + 1.0) * lg_lane)
    kd = k * jnp.exp((float(c - 1) - t_col) * lg_lane)

    ti = lax.broadcasted_iota(I32, (c, c), 0)
    si = lax.broadcasted_iota(I32, (c, c), 1)
    causal = ti >= si
    diff = jnp.where(causal, ti - si, 0).astype(F32)
    lane_qk = lax.broadcasted_iota(I32, (c, 128), 1) >> 5
    lane_v = lax.broadcasted_iota(I32, (c, 256), 1) >> 6
    kb = k.astype(BF16)
    vb = v.astype(BF16)
    y = _dot(qd.astype(BF16), state_ref[...].astype(BF16))
    for h in range(N_HEADS):
        qm = jnp.where(lane_qk == h, q, 0.0).astype(BF16)
        s = _dot_nt(qm, kb)
        dec = jnp.where(causal, jnp.exp(diff * log_gamma[h]), 0.0)
        yh = _dot((s * dec).astype(BF16), vb)
        y = y + jnp.where(lane_v == h, yh, 0.0)

    row_h = lax.broadcasted_iota(I32, (128, 256), 0) >> 5
    col_h = lax.broadcasted_iota(I32, (128, 256), 1) >> 6
    lg_row = jnp.zeros((128, 256), F32)
    for h in range(N_HEADS):
        lg_row = jnp.where(row_h == h, log_gamma[h], lg_row)
    new = _dot_tn(kd.astype(BF16), vb)
    state_ref[...] = state_ref[...] * jnp.exp(float(c) * lg_row) + jnp.where(row_h == col_h, new, 0.0)

    ms = _dot_exact_rhs(y * y, _block_diag_ones(256, 6)) * (1.0 / HEAD_DIM)
    o_ref[...] = y * lax.rsqrt(ms + NORM_EPS) * _silu(g)


def _retention(ret_p, cos32, sin32):
    b, s, _ = ret_p.shape
    c = min(RET_CHUNK, s)
    return pl.pallas_call(
        functools.partial(_ret_kernel, chunk=c),
        grid=(b, s // c),
        in_specs=[pl.BlockSpec((None, c, 768), lambda i, j: (i, j, 0)),
                  pl.BlockSpec((c, 128), lambda i, j: (j, 0)),
                  pl.BlockSpec((c, 128), lambda i, j: (j, 0))],
        out_specs=pl.BlockSpec((None, c, 256), lambda i, j: (i, j, 0)),
        out_shape=jax.ShapeDtypeStruct((b, s, 256), F32),
        scratch_shapes=[pltpu.VMEM((128, 256), F32)],
        compiler_params=_cparams("parallel", "arbitrary"), name="retention",
    )(ret_p, cos32, sin32)


def _ssd_kernel(blk_ref, cw_ref, cb_ref, dtb_ref, aneg_ref, dsk_ref, nw_ref, o_ref,
                state_ref, tail_ref, *, chunk):
    c = chunk

    @pl.when(pl.program_id(1) == 0)
    def _():
        state_ref[...] = jnp.zeros_like(state_ref)
        tail_ref[...] = jnp.zeros_like(tail_ref)

    z = blk_ref[:, 0:256]
    xbc = blk_ref[:, 256:1024]
    dt_raw = blk_ref[:, 1024:1152]

    tail = tail_ref[...]
    conv = xbc * cw_ref[SSM_CONV - 1:SSM_CONV, :] + cb_ref[...]
    for m in range(1, SSM_CONV):
        conv = conv + _shift_rows(tail, xbc, m) * cw_ref[SSM_CONV - 1 - m:SSM_CONV - m, :]
    tail_ref[...] = xbc[c - SUBLANES:c, :]
    xbc = _silu(conv)
    xs = xbc[:, 0:256]
    bm = xbc[:, 256:512]
    cm = xbc[:, 512:768]

    u = dt_raw + dtb_ref[...]
    dt = jnp.maximum(u, 0.0) + jnp.log1p(jnp.exp(-jnp.abs(u)))
    la = dt * aneg_ref[...]
    cum = _cumsum_rows(la)
    cum_t = cum.T
    cum_last = cum[c - 1:c, :]

    dt_lane = _head_select([dt[:, h:h + 1] for h in range(N_HEADS)], 6, 256)
    cum_lane = _head_select([cum[:, h:h + 1] for h in range(N_HEADS)], 6, 256)
    end_lane = _head_select([jnp.broadcast_to(cum_last[:, h:h + 1], (c, 1)) for h in range(N_HEADS)], 6, 256)
    xdt = xs * dt_lane
    x_end = (xdt * jnp.exp(end_lane - cum_lane)).astype(BF16)
    xdt_b = xdt.astype(BF16)
    e_lane = jnp.exp(cum_lane)
    chunk_decay = jnp.exp(end_lane[0:1, :])

    ti = lax.broadcasted_iota(I32, (c, c), 0)
    si = lax.broadcasted_iota(I32, (c, c), 1)
    causal = ti >= si
    lane_half = lax.broadcasted_iota(I32, (c, 128), 1) >> 6
    ys = []
    for grp in range(2):
        sl = slice(128 * grp, 128 * grp + 128)
        cg = cm[:, sl].astype(BF16)
        bg = bm[:, sl].astype(BF16)
        s = _dot_nt(cg, bg)
        parts = []
        for hh in range(2):
            h = 2 * grp + hh
            seg = jnp.minimum(cum[:, h:h + 1] - cum_t[h:h + 1, :], 0.0)
            dec = jnp.where(causal, jnp.exp(seg), 0.0)
            parts.append(_dot((s * dec).astype(BF16), xdt_b[:, sl]))
        y_g = jnp.where(lane_half == 0, parts[0], parts[1])
        st = state_ref[grp]
        y_g = y_g + e_lane[:, sl] * _dot(cg, st.astype(BF16))
        state_ref[grp] = st * chunk_decay[:, sl] + _dot_tn(bg, x_end[:, sl])
        ys.append(y_g)
    y = jnp.concatenate(ys, axis=1)
    y = (y + xs * dsk_ref[...]) * _silu(z)
    o_ref[...] = _rms(y, nw_ref[...])


def _ssd(ssm_p, conv_w, conv_b, dt_bias, a_log, d_skip, norm_w):
    b, s, _ = ssm_p.shape
    c = min(SSD_CHUNK, s)
    pad4 = lambda v: jnp.pad(v.astype(F32), (0, 128 - N_HEADS)).reshape(1, 128)
    params = [conv_w.astype(F32), conv_b.reshape(1, SSM_XBC), pad4(dt_bias),
              pad4(-jnp.exp(a_log.astype(F32))),
              jnp.repeat(d_skip, HEAD_DIM).reshape(1, 256), norm_w.reshape(1, 256)]
    full = lambda a: pl.BlockSpec(a.shape, lambda i, j: (0,) * a.ndim)
    return pl.pallas_call(
        functools.partial(_ssd_kernel, chunk=c),
        grid=(b, s // c),
        in_specs=[pl.BlockSpec((None, c, 1152), lambda i, j: (i, j, 0))] + [full(p) for p in params],
        out_specs=pl.BlockSpec((None, c, 256), lambda i, j: (i, j, 0)),
        out_shape=jax.ShapeDtypeStruct((b, s, 256), F32),
        scratch_shapes=[pltpu.VMEM((2, 128, 128), F32), pltpu.VMEM((SUBLANES, SSM_XBC), F32)],
        compiler_params=_cparams("parallel", "arbitrary"), name="ssd",
    )(ssm_p, *params)


def _rwkv_chunk(f, tail, m0, prm, c):
    mu, w0, w2, a0, a2, g2, k_k, k_a, r_k, ln_w, ln_b = prm
    n = N_HEADS * c
    prev = _shift_rows(tail, f, 1)
    x = f + (prev - f) * mu
    r = x[:, 0:256]
    k = x[:, 256:512]
    v = x[:, 512:768]
    lora = x[:, 768:896]
    gl = x[:, 896:1024]

    logw = -RW_DECAY_SCALE * jax.nn.sigmoid(w0 + _dot3(jnp.tanh(lora), w2))
    a = jax.nn.sigmoid(a0 + _dot3(lora, a2))
    gate = _dot(jax.nn.sigmoid(gl).astype(BF16), g2)
    bd = _block_diag_ones(256, 6)
    kk = k * k_k
    kk = kk * lax.rsqrt(_dot_exact_rhs(kk * kk, bd) + 1e-12)
    k2 = k * (1.0 + (a - 1.0) * k_a)

    cum = _cumsum_rows(logw)
    cum_end = cum[c - 1:c, :]
    g_inc = jnp.exp(cum)
    g_exc = jnp.exp(cum - logw)
    g_inv = jnp.exp(-cum)
    g_end = jnp.exp(cum_end - cum)

    lane_h = lax.broadcasted_iota(I32, (c, 256), 1) >> 6

    def stack(t):
        return jnp.concatenate([jnp.where(lane_h == h, t, 0.0) for h in range(N_HEADS)], axis=0).astype(BF16)

    a_s = stack(-kk * g_exc)
    b_s = stack(kk * a * g_inv)
    k_s = stack(k2 * g_inv)
    r_s = stack(r * g_inc)
    v_s = stack(v)
    b_end = stack(kk * a * g_end)
    k_end = stack(k2 * g_end)

    row = lax.broadcasted_iota(I32, (n, n), 0)
    col = lax.broadcasted_iota(I32, (n, n), 1)
    strict = row > col
    incl = row >= col
    l_ab = jnp.where(strict, _dot_nt(a_s, b_s), 0.0)
    l_ak = jnp.where(strict, _dot_nt(a_s, k_s), 0.0).astype(BF16)
    p_rb = jnp.where(incl, _dot_nt(r_s, b_s), 0.0).astype(BF16)
    p_rk = jnp.where(incl, _dot_nt(r_s, k_s), 0.0).astype(BF16)

    m0b = m0.astype(BF16)
    u = _dot(a_s, m0b) + _dot(l_ak, v_s)
    p = l_ab.astype(BF16)
    steps = max(1, int(math.ceil(math.log2(c))))
    for i in range(steps):
        u = u + _dot(p, u.astype(BF16))
        if i + 1 < steps:
            p = _dot(p, p).astype(BF16)
    ub = u.astype(BF16)
    y_s = _dot(r_s, m0b) + _dot(p_rb, ub) + _dot(p_rk, v_s)
    y = y_s[0:c]
    for h in range(1, N_HEADS):
        y = y + y_s[h * c:(h + 1) * c]

    ones = jnp.ones((c, 128), BF16)
    lw_hi, lw_lo = _split(logw)
    lw_lo2 = (logw - lw_hi.astype(F32) - lw_lo.astype(F32)).astype(BF16)
    tot = _dot_tn(lw_hi, ones) + _dot_tn(lw_lo, ones) + _dot_tn(lw_lo2, ones)
    g_col = jnp.exp(jnp.concatenate([tot, tot], axis=1))
    m_new = m0 * g_col + _dot_tn(b_end, ub) + _dot_tn(k_end, v_s)

    inv_n = 1.0 / HEAD_DIM
    mean = _dot_exact_rhs(y, bd) * inv_n
    yc = y - mean
    var = _dot_exact_rhs(yc * yc, bd) * inv_n
    yn = yc * lax.rsqrt(var + RW_GN_EPS) * ln_w + ln_b
    bonus = _dot_exact_rhs(r * k2 * r_k, bd) * v
    return (yn + bonus) * gate, m_new


def _rwkv_kernel(blk_ref, mu_ref, w0_ref, w2_ref, a0_ref, a2_ref, g2_ref, kk_ref, ka_ref, rk_ref,
                 lnw_ref, lnb_ref, o_ref, state_ref, prev_ref, *, chunk, n_seq):
    c = chunk

    @pl.when(pl.program_id(1) == 0)
    def _():
        state_ref[...] = jnp.zeros_like(state_ref)
        prev_ref[...] = jnp.zeros_like(prev_ref)

    prm = (mu_ref[...], w0_ref[...], w2_ref[...], a0_ref[...], a2_ref[...], g2_ref[...].astype(BF16),
           kk_ref[...], ka_ref[...], rk_ref[...], lnw_ref[...], lnb_ref[...])
    for i in range(n_seq):
        f = blk_ref[i]
        out, m_new = _rwkv_chunk(f, prev_ref[i], state_ref[i], prm, c)
        prev_ref[i] = f[c - SUBLANES:c, :]
        state_ref[i] = m_new
        o_ref[i] = out


def _rwkv(rw_p, mu, w0, w2, a0, a2, g2, k_k, k_a, r_k, ln_w, ln_b):
    b, s, _ = rw_p.shape
    c = min(RW_CHUNK, s)
    row = lambda v: v.astype(F32).reshape(1, -1)
    w2p = jnp.concatenate([w2, jnp.zeros_like(w2)], axis=0)
    a2p = jnp.concatenate([jnp.zeros_like(a2), a2], axis=0)
    params = [row(mu), row(w0), w2p, row(a0), a2p, g2, row(k_k), row(k_a), row(r_k), row(ln_w), row(ln_b)]
    full = lambda a: pl.BlockSpec(a.shape, lambda i, j: (0,) * a.ndim)
    n_seq = RW_SEQS_PER_STEP if b % RW_SEQS_PER_STEP == 0 else 1
    return pl.pallas_call(
        functools.partial(_rwkv_kernel, chunk=c, n_seq=n_seq),
        grid=(b // n_seq, s // c),
        in_specs=[pl.BlockSpec((n_seq, c, 1024), lambda i, j: (i, j, 0))] + [full(p) for p in params],
        out_specs=pl.BlockSpec((n_seq, c, 256), lambda i, j: (i, j, 0)),
        out_shape=jax.ShapeDtypeStruct((b, s, 256), F32),
        scratch_shapes=[pltpu.VMEM((n_seq, 256, 256), F32), pltpu.VMEM((n_seq, SUBLANES, 1024), F32)],
        compiler_params=_cparams("parallel", "arbitrary"), name="rwkv7",
    )(rw_p, *params)


def _dsa_kernel(q_ref, iq_ref, kv_ref, iki_ref, cos32_ref, sin32_ref, cos64_ref, sin64_ref, g_ref,
                o_ref, ik3_ref, kx_ref, vx_ref, keys_ref, hi_ref, lo_ref, m_ref, acc_ref, s_ref, p_ref,
                *, seq, topk, kc):
    qb = DSA_QBLOCK
    j = pl.program_id(1)
    n_kc = (j * qb + qb + kc - 1) // kc

    @pl.when(j == 0)
    def _():
        lane = lax.broadcasted_iota(I32, (seq, 128), 1)
        kv = kv_ref[...]
        kr = kv * cos64_ref[:, 0:128] + _rot_half(kv, HEAD_DIM // 2) * sin64_ref[:, 0:128]
        kx_ref[...] = jnp.where(lane < HEAD_DIM, kr, 0.0).T.astype(BF16)
        vx_ref[...] = jnp.where(lane < HEAD_DIM, pltpu.roll(kv, 64, 1), 1.0).astype(BF16)
        ik = jnp.where(lane < IDX_DIM, iki_ref[...], 0.0)
        ik = ik * lax.rsqrt(jnp.sum(ik * ik, axis=-1, keepdims=True) * (1.0 / IDX_DIM) + NORM_EPS) * g_ref[...]
        ik = ik * cos32_ref[...] + _rot_half(ik, IDX_DIM // 2) * sin32_ref[...]
        ik = jnp.where(lane < IDX_DIM, ik, 0.0)
        hi, lo = _split(ik)
        hi, lo = hi.astype(F32), lo.astype(F32)
        ik3_ref[...] = (hi + pltpu.roll(hi, 32, 1) + pltpu.roll(lo, 64, 1)).T.astype(BF16)

    q0 = pl.multiple_of(j * qb, qb)
    q_pos = q0 + lax.broadcasted_iota(I32, (qb, 1), 0)
    lane = lax.broadcasted_iota(I32, (qb, 128), 1)

    iq = iq_ref[...]
    iq = iq * cos32_ref[pl.ds(q0, qb), :] + _rot_half(iq, IDX_DIM // 2) * sin32_ref[pl.ds(q0, qb), :]
    iq_hi, iq_lo = _split(iq)
    iq_hi, iq_lo = iq_hi.astype(F32), iq_lo.astype(F32)
    iq_lhs = []
    for h in range(IDX_HEADS):
        mine = (lane >> 5) == h
        a0 = jnp.where(mine, iq_hi, 0.0)
        b0 = jnp.where(mine, iq_lo, 0.0)
        if h:
            a0 = pltpu.roll(a0, 128 - 32 * h, 1)
            b0 = pltpu.roll(b0, 128 - 32 * h, 1)
        iq_lhs.append((a0 + pltpu.roll(b0, 32, 1) + pltpu.roll(a0, 64, 1)).astype(BF16))
    iq_stack = jnp.concatenate(iq_lhs, axis=0)
    iw = iki_ref[pl.ds(q0, qb), :] * (IDX_HEADS ** -0.5 * IDX_DIM ** -0.5)
    iw_cols = [iw[:, IDX_DIM + h:IDX_DIM + h + 1] for h in range(IDX_HEADS)]
    col_iota = lax.broadcasted_iota(I32, (qb, kc), 1)

    def score_body(ci, carry):
        off = pl.multiple_of(ci * kc, kc)
        rel = jnp.maximum(_dot(iq_stack, ik3_ref[:, pl.ds(off, kc)]), 0.0)
        sc = rel[0:qb] * iw_cols[0]
        for h in range(1, IDX_HEADS):
            sc = sc + rel[h * qb:(h + 1) * qb] * iw_cols[h]
        sc = sc + 0.0
        sc = jnp.where(off + col_iota <= q_pos, sc, -jnp.inf)
        bits = lax.bitcast_convert_type(sc, I32)
        key = bits ^ ((bits >> 31) & 0x7FFFFFFF)
        keys_ref[:, pl.ds(off, kc)] = key
        hi_ref[:, pl.ds(off, kc)] = (key >> 16).astype(I16)
        return carry

    lax.fori_loop(0, n_kc, score_body, 0)

    def count(ref, pred, one, zero):
        def body(ci, acc):
            off = pl.multiple_of(ci * kc, kc)
            x = jnp.where(pred(ref[:, pl.ds(off, kc)], off), one, zero)
            part = x[:, 0:128]
            for t in range(1, kc // 128):
                part = part + x[:, t * 128:(t + 1) * 128]
            return acc + part
        acc = lax.fori_loop(0, n_kc, body, jnp.zeros((qb, 128), one.dtype))
        return jnp.sum(acc.astype(F32), axis=1, keepdims=True)

    def count16(ref, pred):
        return count(ref, pred, jnp.int16(1), jnp.int16(0))

    def count32(pred):
        return count(keys_ref, pred, jnp.float32(1), jnp.float32(0))

    def search16(ref, want, known_u=0, n_bits=16):
        def bit_body(i, t_u):
            cand_u = t_u | lax.shift_left(jnp.int32(1), jnp.int32(n_bits - 1) - i)
            cand = (cand_u - 32768).astype(I16)
            cnt = count16(ref, lambda kk_, off: kk_ >= cand)
            return jnp.where(cnt >= want, cand_u, t_u)
        return lax.fori_loop(0, n_bits, bit_body, jnp.full((qb, 1), known_u, I32)) - 32768

    t_hi = search16(hi_ref, float(topk))
    t_hi16 = t_hi.astype(I16)
    want_lo = float(topk) - count16(hi_ref, lambda kk_, off: kk_ > t_hi16)

    def lo_body(ci, carry):
        off = pl.multiple_of(ci * kc, kc)
        low = ((keys_ref[:, pl.ds(off, kc)] & 0xFFFF) - 32768).astype(I16)
        lo_ref[:, pl.ds(off, kc)] = jnp.where(hi_ref[:, pl.ds(off, kc)] == t_hi16, low, jnp.int16(-32768))
        return carry

    lax.fori_loop(0, n_kc, lo_body, 0)
    t_lo = search16(lo_ref, want_lo)
    tau = lax.shift_left(t_hi, 16) | (t_lo + 32768)
    need = float(topk) - count32(lambda kk_, off: kk_ > tau)

    idx_bits = max(1, int(math.ceil(math.log2(seq))))

    def tie_body(ci, carry):
        off = pl.multiple_of(ci * kc, kc)
        rev = (seq - 1) - (off + col_iota)
        lo_ref[:, pl.ds(off, kc)] = jnp.where(keys_ref[:, pl.ds(off, kc)] == tau, rev, -1).astype(I16)
        return carry

    lax.fori_loop(0, n_kc, tie_body, 0)
    c_star = (seq - 1) - search16(lo_ref, need, known_u=32768, n_bits=idx_bits)

    q = q_ref[...]
    q = (q * cos64_ref[pl.ds(q0, qb), :] + _rot_half(q, HEAD_DIM // 2) * sin64_ref[pl.ds(q0, qb), :]) * (HEAD_DIM ** -0.5)
    q_lhs = []
    for h in range(N_HEADS):
        half = q[:, 128 * (h >> 1):128 * (h >> 1) + 128]
        half = jnp.where((lane >> 6) == (h & 1), half, 0.0)
        q_lhs.append((pltpu.roll(half, 64, 1) if h & 1 else half).astype(BF16))
    neg = -1e30
    q_stack = jnp.concatenate(q_lhs, axis=0)
    m_ref[...] = jnp.full_like(m_ref, neg)
    acc_ref[...] = jnp.zeros_like(acc_ref)

    def attn_body(ci, carry):
        off = pl.multiple_of(ci * kc, kc)
        kk_ = keys_ref[:, pl.ds(off, kc)]
        colp = off + col_iota
        sel = ((kk_ > tau) | ((kk_ == tau) & (colp <= c_star))) & (colp <= q_pos)
        bias = jnp.where(sel, 0.0, neg)
        kx = kx_ref[:, pl.ds(off, kc)]
        vx = vx_ref[pl.ds(off, kc), :]
        s_ref[...] = _dot(q_stack, kx) + jnp.concatenate([bias] * N_HEADS, axis=0)
        m_old = m_ref[...]
        m_new = jnp.maximum(m_old, jnp.max(s_ref[...], axis=-1, keepdims=True))
        for t in range(kc // LANES):
            sl = slice(t * LANES, (t + 1) * LANES)
            p_ref[:, sl] = jnp.exp(s_ref[:, sl] - m_new).astype(BF16)
        acc_ref[...] = jnp.exp(m_old - m_new) * acc_ref[...] + _dot(p_ref[...], vx)
        m_ref[...] = m_new
        return carry

    lax.fori_loop(0, n_kc, attn_body, 0)
    halves = []
    for pair in range(N_HEADS // 2):
        outs = []
        for h in (2 * pair, 2 * pair + 1):
            a = acc_ref[h * qb:(h + 1) * qb]
            outs.append(a / pltpu.roll(a, 64, 1))
        halves.append(jnp.where(lane < HEAD_DIM, outs[0], pltpu.roll(outs[1], 64, 1)))
    o_ref[...] = jnp.concatenate(halves, axis=1)


def _dsa(dsa_p, idx_k_norm, cos32, sin32, cos64, sin64):
    b, s, _ = dsa_p.shape
    qb = DSA_QBLOCK
    kc = min(DSA_KEY_CHUNK, s)
    topk = min(DSA_TOPK_MAX, s // 4)
    assert s % kc == 0 and s <= 2 ** 15, "key positions are searched as packed 16-bit values"
    gpad = jnp.pad(idx_k_norm.astype(F32), (0, 128 - IDX_DIM)).reshape(1, 128)
    full = lambda a: pl.BlockSpec(a.shape, lambda i, j: (0,) * a.ndim)
    return pl.pallas_call(
        functools.partial(_dsa_kernel, seq=s, topk=topk, kc=kc),
        grid=(b, s // qb),
        in_specs=[pl.BlockSpec((None, qb, 256), lambda i, j: (i, j, 0)),
                  pl.BlockSpec((None, qb, 128), lambda i, j: (i, j, 2)),
                  pl.BlockSpec((None, s, 128), lambda i, j: (i, 0, 3)),
                  pl.BlockSpec((None, s, 128), lambda i, j: (i, 0, 4)),
                  full(cos32), full(sin32), full(cos64), full(sin64), full(gpad)],
        out_specs=pl.BlockSpec((None, qb, 256), lambda i, j: (i, j, 0)),
        out_shape=jax.ShapeDtypeStruct((b, s, 256), F32),
        scratch_shapes=[pltpu.VMEM((128, s), BF16), pltpu.VMEM((128, s), BF16), pltpu.VMEM((s, 128), BF16),
                        pltpu.VMEM((qb, s), I32), pltpu.VMEM((qb, s), I16), pltpu.VMEM((qb, s), I16),
                        pltpu.VMEM((N_HEADS * qb, LANES), F32), pltpu.VMEM((N_HEADS * qb, LANES), F32),
                        pltpu.VMEM((N_HEADS * qb, kc), F32), pltpu.VMEM((N_HEADS * qb, kc), BF16)],
        compiler_params=_cparams("parallel", "arbitrary"), name="dsa",
    )(dsa_p, dsa_p, dsa_p, dsa_p, cos32, sin32, cos64, sin64, gpad)


def _mix_cross_kernel(h_ref, o0_ref, o1_ref, o2_ref, o3_ref, wout_ref, g_ref, wq_ref, k_ref, v_ref, wo_ref,
                      out_ref):
    h = h_ref[...]
    for i, o_ref in enumerate((o0_ref, o1_ref, o2_ref, o3_ref)):
        h = h + _dot(o_ref[...].astype(BF16), wout_ref[i * GROUP_WIDTH:(i + 1) * GROUP_WIDTH, :])
    q = _dot(_rms(h, g_ref[...]).astype(BF16), wq_ref[...]) * (X_HEAD_DIM ** -0.5)
    outs = []
    for hd in range(X_HEADS):
        sl = slice(hd * X_HEAD_DIM, (hd + 1) * X_HEAD_DIM)
        s = _dot_nt(q[:, sl].astype(BF16), k_ref[:, sl])
        p = jnp.exp(s - jnp.max(s, axis=-1, keepdims=True))
        p = p / jnp.sum(p, axis=-1, keepdims=True)
        outs.append(_dot(p.astype(BF16), v_ref[:, sl]))
    o = jnp.concatenate(outs, axis=1).astype(BF16)
    out_ref[...] = h + _dot(o, wo_ref[...])


def _mix_cross(h2d, outs, w_out, g_cross, wq, k_mem, v_mem, wo, seq, tm):
    t, d = h2d.shape
    n_mem = k_mem.shape[1]
    per_seq = seq // tm
    tok = lambda w: pl.BlockSpec((tm, w), lambda i: (i, 0))
    const = lambda a: pl.BlockSpec(a.shape, lambda i: (0,) * a.ndim)
    mem = pl.BlockSpec((None, n_mem, d), lambda i: (i // per_seq, 0, 0))
    return pl.pallas_call(
        _mix_cross_kernel,
        grid=(t // tm,),
        in_specs=[tok(d)] + [tok(GROUP_WIDTH)] * 4 + [const(w_out), pl.BlockSpec((1, d), lambda i: (0, 0)),
                                                     const(wq), mem, mem, const(wo)],
        out_specs=tok(d),
        out_shape=jax.ShapeDtypeStruct((t, d), F32),
        compiler_params=_cparams("parallel"), name="mix_cross",
    )(h2d, *outs, w_out, g_cross.reshape(1, d), wq, k_mem, v_mem, wo)


def _ffn_kernel(h_ref, g_ref, wg_ref, wv_ref, cw_ref, cb_ref, wd_ref, gf_ref, out_ref, tail_ref, acc_ref,
                *, per_seq, n_chunks, final_norm):
    tm = h_ref.shape[0]

    @pl.when(pl.program_id(0) % per_seq == 0)
    def _():
        tail_ref[...] = jnp.zeros_like(tail_ref)

    h = h_ref[...]
    xb = _rms(h, g_ref[...]).astype(BF16)
    acc_ref[...] = jnp.zeros_like(acc_ref)

    def body(ci, carry):
        gate = _dot(xb, wg_ref[ci])
        val = _dot(xb, wv_ref[ci])
        tail = tail_ref[ci]
        cw = cw_ref[ci]
        conv = gate * cw[FFN_CONV - 1:FFN_CONV, :] + cb_ref[ci]
        for m in range(1, FFN_CONV):
            conv = conv + _shift_rows(tail, gate, m) * cw[FFN_CONV - 1 - m:FFN_CONV - m, :]
        tail_ref[ci] = gate[tm - SUBLANES:tm, :]
        acc_ref[...] += _dot((_silu(conv) * val).astype(BF16), wd_ref[ci])
        return carry

    lax.fori_loop(0, n_chunks, body, 0, unroll=True)
    out = h + acc_ref[...]
    if final_norm:
        out = _rms(out, gf_ref[...])
    out_ref[...] = out


def _ffn(h2d, g_ffn, w_up, conv_w, conv_b, w_down, g_final, seq, tm, final_norm):
    t, d = h2d.shape
    fc = FF_CHUNK
    nck = D_FF // fc
    chunked = lambda w: w.reshape(w.shape[0], nck, fc).transpose(1, 0, 2)
    wg = chunked(w_up[:, :D_FF]).astype(BF16)
    wv = chunked(w_up[:, D_FF:]).astype(BF16)
    cw = chunked(conv_w.astype(F32))
    cb = conv_b.astype(F32).reshape(nck, 1, fc)
    wd = w_down.reshape(nck, fc, d).astype(BF16)
    const = lambda a: pl.BlockSpec(a.shape, lambda i: (0,) * a.ndim)
    row = pl.BlockSpec((1, d), lambda i: (0, 0))
    return pl.pallas_call(
        functools.partial(_ffn_kernel, per_seq=seq // tm, n_chunks=nck, final_norm=final_norm),
        grid=(t // tm,),
        in_specs=[pl.BlockSpec((tm, d), lambda i: (i, 0)), row, const(wg), const(wv), const(cw), const(cb),
                  const(wd), row],
        out_specs=pl.BlockSpec((tm, d), lambda i: (i, 0)),
        out_shape=jax.ShapeDtypeStruct((t, d), F32),
        scratch_shapes=[pltpu.VMEM((nck, SUBLANES, fc), F32), pltpu.VMEM((tm, d), F32)],
        compiler_params=_cparams("arbitrary"), name="conv_glu",
    )(h2d, g_ffn.reshape(1, d), wg, wv, cw, cb, wd, g_final.reshape(1, d))


def _rope_tables(seq, head_dim, width):
    half = head_dim // 2
    inv = ROPE_THETA ** (-jnp.arange(half, dtype=F32) / half)
    ang = jnp.arange(seq, dtype=F32)[:, None] * inv[None, :]
    reps = width // half
    return jnp.tile(jnp.cos(ang), (1, reps)), jnp.tile(jnp.sin(ang), (1, reps))


def _split_w_in(w):
    ret = w[:, 0:768]
    rw = w[:, 768:1792]
    ssm = jnp.pad(w[:, 1792:2820], ((0, 0), (0, 1152 - 1028)))
    d = w[:, 2820:3368]
    dsa = jnp.concatenate([d[:, 0:256], d[:, 384:512], d[:, 256:384], d[:, 512:548],
                           jnp.zeros((w.shape[0], 128 - 36), w.dtype)], axis=1)
    return [t.astype(BF16) for t in (ret, rw, ssm, dsa)]


def kernel(x, mem, norm_mix, w_in, rwkv_mu, rwkv_w0, rwkv_w2, rwkv_a0, rwkv_a2, rwkv_g2, rwkv_k_k, rwkv_k_a, rwkv_r_k, rwkv_ln_w, rwkv_ln_b, ssm_conv_w, ssm_conv_b, ssm_dt_bias, ssm_a_log, ssm_d, ssm_norm, idx_k_norm, w_out, norm_cross, norm_mem, wq_x, wk_x, wv_x, wo_x, norm_ffn, w_up, ffn_conv_w, ffn_conv_b, w_down, norm_final):
    b, s, d = x.shape
    n_mem = mem.shape[1]
    depth = w_in.shape[0]
    t = b * s
    tm = min(512, s)
    cos32, sin32 = _rope_tables(s, RET_QK_DIM, 128)
    cos64, sin64 = _rope_tables(s, HEAD_DIM, 256)
    mem2d = mem.reshape(b * n_mem, d)
    h = x.reshape(t, d)
    for l in range(depth):
        ret_p, rw_p, ssm_p, dsa_p = _norm_matmul(h, norm_mix[l], _split_w_in(w_in[l]), [F32] * 4, tm)
        o_ret = _retention(ret_p.reshape(b, s, -1), cos32, sin32)
        o_rw = _rwkv(rw_p.reshape(b, s, -1), rwkv_mu[l], rwkv_w0[l], rwkv_w2[l], rwkv_a0[l], rwkv_a2[l],
                     rwkv_g2[l], rwkv_k_k[l], rwkv_k_a[l], rwkv_r_k[l], rwkv_ln_w[l], rwkv_ln_b[l])
        o_ssm = _ssd(ssm_p.reshape(b, s, -1), ssm_conv_w[l], ssm_conv_b[l], ssm_dt_bias[l], ssm_a_log[l],
                     ssm_d[l], ssm_norm[l])
        o_dsa = _dsa(dsa_p.reshape(b, s, -1), idx_k_norm[l], cos32, sin32, cos64, sin64)
        k_mem, v_mem = _norm_matmul(mem2d, norm_mem[l], [wk_x[l].astype(BF16), wv_x[l].astype(BF16)],
                                    [BF16, BF16], min(512, b * n_mem))
        outs = [o.reshape(t, GROUP_WIDTH) for o in (o_ret, o_rw, o_ssm, o_dsa)]
        h = _mix_cross(h, outs, w_out[l].astype(BF16), norm_cross[l], wq_x[l].astype(BF16),
                       k_mem.reshape(b, n_mem, d), v_mem.reshape(b, n_mem, d), wo_x[l].astype(BF16), s, tm)
        h = _ffn(h, norm_ffn[l], w_up[l], ffn_conv_w[l], ffn_conv_b[l], w_down[l], norm_final, s, tm,
                 final_norm=(l == depth - 1))
    return h.reshape(b, s, d)
```

```python
import functools
import math

import jax
import jax.numpy as jnp
import numpy as np
from jax import lax
from jax.experimental import pallas as pl
from jax.experimental.pallas import tpu as pltpu

F32 = jnp.float32
BF16 = jnp.bfloat16
I32 = jnp.int32
I16 = jnp.int16

D_MODEL = 1024
GROUP_WIDTH = 256
HEAD_DIM = 64
N_HEADS = 4
RET_QK_DIM = 32
RW_DECAY_SCALE = math.exp(-0.5)
RW_GN_EPS = 64e-5
SSM_STATE = 128
SSM_CONV = 4
SSM_XBC = 768
IDX_HEADS = 4
IDX_DIM = 32
DSA_TOPK_MAX = 256
DSA_QBLOCK = 256
X_HEADS = 4
X_HEAD_DIM = 256
D_FF = 2816
FFN_CONV = 3
ROPE_THETA = 10000.0
NORM_EPS = 1e-6

VMEM_LIMIT_BYTES = 52 * 1024 * 1024
SUBLANES = 8
LANES = 128

RET_CHUNK = 256
SSD_CHUNK = 128
RW_CHUNK = 64
RW_SEQS_PER_STEP = 2
DSA_KEY_CHUNK = 512
FF_CHUNK = 256


def _cparams(*sem):
    return pltpu.CompilerParams(dimension_semantics=sem, vmem_limit_bytes=VMEM_LIMIT_BYTES)


def _dot(a, b):
    return jnp.dot(a, b, preferred_element_type=F32)


def _dot_nt(a, b):
    return lax.dot_general(a, b, (((1,), (1,)), ((), ())), preferred_element_type=F32)


def _dot_tn(a, b):
    return lax.dot_general(a, b, (((0,), (0,)), ((), ())), preferred_element_type=F32)


def _split(a):
    hi = a.astype(BF16)
    lo = (a - hi.astype(F32)).astype(BF16)
    return hi, lo


def _dot_exact_rhs(a, b_bf16):
    hi, lo = _split(a)
    return _dot(hi, b_bf16) + _dot(lo, b_bf16)


def _dot3(a, b):
    ah, al = _split(a)
    bh, bl = _split(b)
    return _dot(ah, bh) + (_dot(ah, bl) + _dot(al, bh))


def _rms(x, g):
    return x * lax.rsqrt(jnp.mean(x * x, axis=-1, keepdims=True) + NORM_EPS) * g


def _silu(x):
    return x * jax.nn.sigmoid(x)


def _block_diag_ones(n, group_shift):
    r = lax.broadcasted_iota(I32, (n, n), 0) >> group_shift
    c = lax.broadcasted_iota(I32, (n, n), 1) >> group_shift
    return jnp.where(r == c, 1.0, 0.0).astype(BF16)


def _rot_half(x, half):
    n = x.shape[-1]
    lane = lax.broadcasted_iota(I32, x.shape, 1)
    first = (lane & (2 * half - 1)) < half
    return jnp.where(first, -pltpu.roll(x, n - half, 1), pltpu.roll(x, half, 1))


def _shift_rows(tail8, x, m):
    c = x.shape[0]
    xx = jnp.concatenate([tail8, x], axis=0)
    return pltpu.roll(xx, m, 0)[SUBLANES:SUBLANES + c]


def _cumsum_rows(x):
    c = x.shape[0]
    row = lax.broadcasted_iota(I32, x.shape, 0)
    d = 1
    while d < c:
        x = x + jnp.where(row >= d, pltpu.roll(x, d, 0), 0.0)
        d *= 2
    return x


def _head_select(cols, shift, width):
    c = cols[0].shape[0]
    lane_head = lax.broadcasted_iota(I32, (c, width), 1) >> shift
    out = jnp.zeros((c, width), F32)
    for h, col in enumerate(cols):
        out = jnp.where(lane_head == h, col, out)
    return out


def _norm_matmul_kernel(x_ref, g_ref, *refs, n_w):
    w_refs, o_refs = refs[:n_w], refs[n_w:]
    xb = _rms(x_ref[...], g_ref[...]).astype(BF16)
    for w_ref, o_ref in zip(w_refs, o_refs):
        o_ref[...] = _dot(xb, w_ref[...]).astype(o_ref.dtype)


def _norm_matmul(x2d, gain, weights, out_dtypes, tm):
    t, d = x2d.shape
    n_w = len(weights)
    in_specs = [pl.BlockSpec((tm, d), lambda i: (i, 0)), pl.BlockSpec((1, d), lambda i: (0, 0))]
    in_specs += [pl.BlockSpec(w.shape, lambda i: (0, 0)) for w in weights]
    out_specs = [pl.BlockSpec((tm, w.shape[1]), lambda i: (i, 0)) for w in weights]
    out_shape = [jax.ShapeDtypeStruct((t, w.shape[1]), dt) for w, dt in zip(weights, out_dtypes)]
    return pl.pallas_call(
        functools.partial(_norm_matmul_kernel, n_w=n_w),
        grid=(t // tm,), in_specs=in_specs, out_specs=out_specs, out_shape=out_shape,
        compiler_params=_cparams("parallel"), name="norm_matmul",
    )(x2d, gain.reshape(1, d), *weights)


def _ret_kernel(blk_ref, cos_ref, sin_ref, o_ref, state_ref, *, chunk):
    c = chunk

    @pl.when(pl.program_id(1) == 0)
    def _():
        state_ref[...] = jnp.zeros_like(state_ref)

    cos, sin = cos_ref[...], sin_ref[...]
    q = blk_ref[:, 0:128]
    k = blk_ref[:, 128:256]
    v = blk_ref[:, 256:512]
    g = blk_ref[:, 512:768]
    q = q * cos + _rot_half(q, RET_QK_DIM // 2) * sin
    k = (k * cos + _rot_half(k, RET_QK_DIM // 2) * sin) * (RET_QK_DIM ** -0.5)

    log_gamma = [math.log(1.0 - 2.0 ** (-5.0 - h)) for h in range(N_HEADS)]
    lane_k = lax.broadcasted_iota(I32, (1, 128), 1) >> 5
    lg_lane = jnp.zeros((1, 128), F32)
    for h in range(N_HEADS):
        lg_lane = jnp.where(lane_k == h, log_gamma[h], lg_lane)
    t_col = lax.broadcasted_iota(I32, (c, 1), 0).astype(F32)
    qd = q * jnp.exp((t_col + 1.0) * lg_lane)
    kd = k * jnp.exp((float(c - 1) - t_col) * lg_lane)

    ti = lax.broadcasted_iota(I32, (c, c), 0)
    si = lax.broadcasted_iota(I32, (c, c), 1)
    causal = ti >= si
    diff = jnp.where(causal, ti - si, 0).astype(F32)
    lane_qk = lax.broadcasted_iota(I32, (c, 128), 1) >> 5
    lane_v = lax.broadcasted_iota(I32, (c, 256), 1) >> 6
    kb = k.astype(BF16)
    vb = v.astype(BF16)
    y = _dot(qd.astype(BF16), state_ref[...].astype(BF16))
    for h in range(N_HEADS):
        qm = jnp.where(lane_qk == h, q, 0.0).astype(BF16)
        s = _dot_nt(qm, kb)
        dec = jnp.where(causal, jnp.exp(diff * log_gamma[h]), 0.0)
        yh = _dot((s * dec).astype(BF16), vb)
        y = y + jnp.where(lane_v == h, yh, 0.0)

    row_h = lax.broadcasted_iota(I32, (128, 256), 0) >> 5
    col_h = lax.broadcasted_iota(I32, (128, 256), 1) >> 6
    lg_row = jnp.zeros((128, 256), F32)
    for h in range(N_HEADS):
        lg_row = jnp.where(row_h == h, log_gamma[h], lg_row)
    new = _dot_tn(kd.astype(BF16), vb)
    state_ref[...] = state_ref[...] * jnp.exp(float(c) * lg_row) + jnp.where(row_h == col_h, new, 0.0)

    ms = _dot_exact_rhs(y * y, _block_diag_ones(256, 6)) * (1.0 / HEAD_DIM)
    o_ref[...] = y * lax.rsqrt(ms + NORM_EPS) * _silu(g)


def _retention(ret_p, cos32, sin32):
    b, s, _ = ret_p.shape
    c = min(RET_CHUNK, s)
    return pl.pallas_call(
        functools.partial(_ret_kernel, chunk=c),
        grid=(b, s // c),
        in_specs=[pl.BlockSpec((None, c, 768), lambda i, j: (i, j, 0)),
                  pl.BlockSpec((c, 128), lambda i, j: (j, 0)),
                  pl.BlockSpec((c, 128), lambda i, j: (j, 0))],
        out_specs=pl.BlockSpec((None, c, 256), lambda i, j: (i, j, 0)),
        out_shape=jax.ShapeDtypeStruct((b, s, 256), F32),
        scratch_shapes=[pltpu.VMEM((128, 256), F32)],
        compiler_params=_cparams("parallel", "arbitrary"), name="retention",
    )(ret_p, cos32, sin32)


def _ssd_kernel(blk_ref, cw_ref, cb_ref, dtb_ref, aneg_ref, dsk_ref, nw_ref, o_ref,
                state_ref, tail_ref, *, chunk):
    c = chunk

    @pl.when(pl.program_id(1) == 0)
    def _():
        state_ref[...] = jnp.zeros_like(state_ref)
        tail_ref[...] = jnp.zeros_like(tail_ref)

    z = blk_ref[:, 0:256]
    xbc = blk_ref[:, 256:1024]
    dt_raw = blk_ref[:, 1024:1152]

    tail = tail_ref[...]
    conv = xbc * cw_ref[SSM_CONV - 1:SSM_CONV, :] + cb_ref[...]
    for m in range(1, SSM_CONV):
        conv = conv + _shift_rows(tail, xbc, m) * cw_ref[SSM_CONV - 1 - m:SSM_CONV - m, :]
    tail_ref[...] = xbc[c - SUBLANES:c, :]
    xbc = _silu(conv)
    xs = xbc[:, 0:256]
    bm = xbc[:, 256:512]
    cm = xbc[:, 512:768]

    u = dt_raw + dtb_ref[...]
    dt = jnp.maximum(u, 0.0) + jnp.log1p(jnp.exp(-jnp.abs(u)))
    la = dt * aneg_ref[...]
    cum = _cumsum_rows(la)
    cum_t = cum.T
    cum_last = cum[c - 1:c, :]

    dt_lane = _head_select([dt[:, h:h + 1] for h in range(N_HEADS)], 6, 256)
    cum_lane = _head_select([cum[:, h:h + 1] for h in range(N_HEADS)], 6, 256)
    end_lane = _head_select([jnp.broadcast_to(cum_last[:, h:h + 1], (c, 1)) for h in range(N_HEADS)], 6, 256)
    xdt = xs * dt_lane
    x_end = (xdt * jnp.exp(end_lane - cum_lane)).astype(BF16)
    xdt_b = xdt.astype(BF16)
    e_lane = jnp.exp(cum_lane)
    chunk_decay = jnp.exp(end_lane[0:1, :])

    ti = lax.broadcasted_iota(I32, (c, c), 0)
    si = lax.broadcasted_iota(I32, (c, c), 1)
    causal = ti >= si
    lane_half = lax.broadcasted_iota(I32, (c, 128), 1) >> 6
    ys = []
    for grp in range(2):
        sl = slice(128 * grp, 128 * grp + 128)
        cg = cm[:, sl].astype(BF16)
        bg = bm[:, sl].astype(BF16)
        s = _dot_nt(cg, bg)
        parts = []
        for hh in range(2):
            h = 2 * grp + hh
            seg = jnp.minimum(cum[:, h:h + 1] - cum_t[h:h + 1, :], 0.0)
            dec = jnp.where(causal, jnp.exp(seg), 0.0)
            parts.append(_dot((s * dec).astype(BF16), xdt_b[:, sl]))
        y_g = jnp.where(lane_half == 0, parts[0], parts[1])
        st = state_ref[grp]
        y_g = y_g + e_lane[:, sl] * _dot(cg, st.astype(BF16))
        state_ref[grp] = st * chunk_decay[:, sl] + _dot_tn(bg, x_end[:, sl])
        ys.append(y_g)
    y = jnp.concatenate(ys, axis=1)
    y = (y + xs * dsk_ref[...]) * _silu(z)
    o_ref[...] = _rms(y, nw_ref[...])


def _ssd(ssm_p, conv_w, conv_b, dt_bias, a_log, d_skip, norm_w):
    b, s, _ = ssm_p.shape
    c = min(SSD_CHUNK, s)
    pad4 = lambda v: jnp.pad(v.astype(F32), (0, 128 - N_HEADS)).reshape(1, 128)
    params = [conv_w.astype(F32), conv_b.reshape(1, SSM_XBC), pad4(dt_bias),
              pad4(-jnp.exp(a_log.astype(F32))),
              jnp.repeat(d_skip, HEAD_DIM).reshape(1, 256), norm_w.reshape(1, 256)]
    full = lambda a: pl.BlockSpec(a.shape, lambda i, j: (0,) * a.ndim)
    return pl.pallas_call(
        functools.partial(_ssd_kernel, chunk=c),
        grid=(b, s // c),
        in_specs=[pl.BlockSpec((None, c, 1152), lambda i, j: (i, j, 0))] + [full(p) for p in params],
        out_specs=pl.BlockSpec((None, c, 256), lambda i, j: (i, j, 0)),
        out_shape=jax.ShapeDtypeStruct((b, s, 256), F32),
        scratch_shapes=[pltpu.VMEM((2, 128, 128), F32), pltpu.VMEM((SUBLANES, SSM_XBC), F32)],
        compiler_params=_cparams("parallel", "arbitrary"), name="ssd",
    )(ssm_p, *params)


def _rwkv_chunk(f, tail, m0, prm, c):
    mu, w0, w2, a0, a2, g2, k_k, k_a, r_k, ln_w, ln_b = prm
    n = N_HEADS * c
    prev = _shift_rows(tail, f, 1)
    x = f + (prev - f) * mu
    r = x[:, 0:256]
    k = x[:, 256:512]
    v = x[:, 512:768]
    lora = x[:, 768:896]
    gl = x[:, 896:1024]

    logw = -RW_DECAY_SCALE * jax.nn.sigmoid(w0 + _dot3(jnp.tanh(lora), w2))
    a = jax.nn.sigmoid(a0 + _dot3(lora, a2))
    gate = _dot(jax.nn.sigmoid(gl).astype(BF16), g2)
    bd = _block_diag_ones(256, 6)
    kk = k * k_k
    kk = kk * lax.rsqrt(_dot_exact_rhs(kk * kk, bd) + 1e-12)
    k2 = k * (1.0 + (a - 1.0) * k_a)

    cum = _cumsum_rows(logw)
    cum_end = cum[c - 1:c, :]
    g_inc = jnp.exp(cum)
    g_exc = jnp.exp(cum - logw)
    g_inv = jnp.exp(-cum)
    g_end = jnp.exp(cum_end - cum)

    lane_h = lax.broadcasted_iota(I32, (c, 256), 1) >> 6

    def stack(t):
        return jnp.concatenate([jnp.where(lane_h == h, t, 0.0) for h in range(N_HEADS)], axis=0).astype(BF16)

    a_s = stack(-kk * g_exc)
    b_s = stack(kk * a * g_inv)
    k_s = stack(k2 * g_inv)
    r_s = stack(r * g_inc)
    v_s = stack(v)
    b_end = stack(kk * a * g_end)
    k_end = stack(k2 * g_end)

    row = lax.broadcasted_iota(I32, (n, n), 0)
    col = lax.broadcasted_iota(I32, (n, n), 1)
    strict = row > col
    incl = row >= col
    l_ab = jnp.where(strict, _dot_nt(a_s, b_s), 0.0)
    l_ak = jnp.where(strict, _dot_nt(a_s, k_s), 0.0).astype(BF16)
    p_rb = jnp.where(incl, _dot_nt(r_s, b_s), 0.0).astype(BF16)
    p_rk = jnp.where(incl, _dot_nt(r_s, k_s), 0.0).astype(BF16)

    m0b = m0.astype(BF16)
    u = _dot(a_s, m0b) + _dot(l_ak, v_s)
    p = l_ab.astype(BF16)
    steps = max(1, int(math.ceil(math.log2(c))))
    for i in range(steps):
        u = u + _dot(p, u.astype(BF16))
        if i + 1 < steps:
            p = _dot(p, p).astype(BF16)
    ub = u.astype(BF16)
    y_s = _dot(r_s, m0b) + _dot(p_rb, ub) + _dot(p_rk, v_s)
    y = y_s[0:c]
    for h in range(1, N_HEADS):
        y = y + y_s[h * c:(h + 1) * c]

    ones = jnp.ones((c, 128), BF16)
    lw_hi, lw_lo = _split(logw)
    lw_lo2 = (logw - lw_hi.astype(F32) - lw_lo.astype(F32)).astype(BF16)
    tot = _dot_tn(lw_hi, ones) + _dot_tn(lw_lo, ones) + _dot_tn(lw_lo2, ones)
    g_col = jnp.exp(jnp.concatenate([tot, tot], axis=1))
    m_new = m0 * g_col + _dot_tn(b_end, ub) + _dot_tn(k_end, v_s)

    inv_n = 1.0 / HEAD_DIM
    mean = _dot_exact_rhs(y, bd) * inv_n
    yc = y - mean
    var = _dot_exact_rhs(yc * yc, bd) * inv_n
    yn = yc * lax.rsqrt(var + RW_GN_EPS) * ln_w + ln_b
    bonus = _dot_exact_rhs(r * k2 * r_k, bd) * v
    return (yn + bonus) * gate, m_new


def _rwkv_kernel(blk_ref, mu_ref, w0_ref, w2_ref, a0_ref, a2_ref, g2_ref, kk_ref, ka_ref, rk_ref,
                 lnw_ref, lnb_ref, o_ref, state_ref, prev_ref, *, chunk, n_seq):
    c = chunk

    @pl.when(pl.program_id(1) == 0)
    def _():
        state_ref[...] = jnp.zeros_like(state_ref)
        prev_ref[...] = jnp.zeros_like(prev_ref)

    prm = (mu_ref[...], w0_ref[...], w2_ref[...], a0_ref[...], a2_ref[...], g2_ref[...].astype(BF16),
           kk_ref[...], ka_ref[...], rk_ref[...], lnw_ref[...], lnb_ref[...])
    for i in range(n_seq):
        f = blk_ref[i]
        out, m_new = _rwkv_chunk(f, prev_ref[i], state_ref[i], prm, c)
        prev_ref[i] = f[c - SUBLANES:c, :]
        state_ref[i] = m_new
        o_ref[i] = out


def _rwkv(rw_p, mu, w0, w2, a0, a2, g2, k_k, k_a, r_k, ln_w, ln_b):
    b, s, _ = rw_p.shape
    c = min(RW_CHUNK, s)
    row = lambda v: v.astype(F32).reshape(1, -1)
    w2p = jnp.concatenate([w2, jnp.zeros_like(w2)], axis=0)
    a2p = jnp.concatenate([jnp.zeros_like(a2), a2], axis=0)
    params = [row(mu), row(w0), w2p, row(a0), a2p, g2, row(k_k), row(k_a), row(r_k), row(ln_w), row(ln_b)]
    full = lambda a: pl.BlockSpec(a.shape, lambda i, j: (0,) * a.ndim)
    n_seq = RW_SEQS_PER_STEP if b % RW_SEQS_PER_STEP == 0 else 1
    return pl.pallas_call(
        functools.partial(_rwkv_kernel, chunk=c, n_seq=n_seq),
        grid=(b // n_seq, s // c),
        in_specs=[pl.BlockSpec((n_seq, c, 1024), lambda i, j: (i, j, 0))] + [full(p) for p in params],
        out_specs=pl.BlockSpec((n_seq, c, 256), lambda i, j: (i, j, 0)),
        out_shape=jax.ShapeDtypeStruct((b, s, 256), F32),
        scratch_shapes=[pltpu.VMEM((n_seq, 256, 256), F32), pltpu.VMEM((n_seq, SUBLANES, 1024), F32)],
        compiler_params=_cparams("parallel", "arbitrary"), name="rwkv7",
    )(rw_p, *params)


def _dsa_kernel(q_ref, iq_ref, kv_ref, iki_ref, cos32_ref, sin32_ref, cos64_ref, sin64_ref, g_ref,
                o_ref, ik3_ref, kx_ref, vx_ref, keys_ref, hi_ref, lo_ref, m_ref, acc_ref, s_ref, p_ref,
                *, seq, topk, kc):
    qb = DSA_QBLOCK
    j = pl.program_id(1)
    n_kc = (j * qb + qb + kc - 1) // kc

    @pl.when(j == 0)
    def _():
        lane = lax.broadcasted_iota(I32, (seq, 128), 1)
        kv = kv_ref[...]
        kr = kv * cos64_ref[...] + _rot_half(kv, HEAD_DIM // 2) * sin64_ref[...]
        kx_ref[...] = jnp.where(lane < HEAD_DIM, kr, 0.0).T.astype(BF16)
        vx_ref[...] = jnp.where(lane < HEAD_DIM, pltpu.roll(kv, 64, 1), 1.0).astype(BF16)
        ik = jnp.where(lane < IDX_DIM, iki_ref[...], 0.0)
        ik = ik * lax.rsqrt(jnp.sum(ik * ik, axis=-1, keepdims=True) * (1.0 / IDX_DIM) + NORM_EPS) * g_ref[...]
        ik = ik * cos32_ref[...] + _rot_half(ik, IDX_DIM // 2) * sin32_ref[...]
        ik = jnp.where(lane < IDX_DIM, ik, 0.0)
        hi, lo = _split(ik)
        hi, lo = hi.astype(F32), lo.astype(F32)
        ik3_ref[...] = (hi + pltpu.roll(hi, 32, 1) + pltpu.roll(lo, 64, 1)).T.astype(BF16)

    q0 = pl.multiple_of(j * qb, qb)
    q_pos = q0 + lax.broadcasted_iota(I32, (qb, 1), 0)
    lane = lax.broadcasted_iota(I32, (qb, 128), 1)

    iq = iq_ref[...]
    iq = iq * cos32_ref[pl.ds(q0, qb), :] + _rot_half(iq, IDX_DIM // 2) * sin32_ref[pl.ds(q0, qb), :]
    iq_hi, iq_lo = _split(iq)
    iq_hi, iq_lo = iq_hi.astype(F32), iq_lo.astype(F32)
    iq_lhs = []
    for h in range(IDX_HEADS):
        mine = (lane >> 5) == h
        a0 = jnp.where(mine, iq_hi, 0.0)
        b0 = jnp.where(mine, iq_lo, 0.0)
        if h:
            a0 = pltpu.roll(a0, 128 - 32 * h, 1)
            b0 = pltpu.roll(b0, 128 - 32 * h, 1)
        iq_lhs.append((a0 + pltpu.roll(b0, 32, 1) + pltpu.roll(a0, 64, 1)).astype(BF16))
    iq_stack = jnp.concatenate(iq_lhs, axis=0)
    iw = iki_ref[pl.ds(q0, qb), :] * (IDX_HEADS ** -0.5 * IDX_DIM ** -0.5)
    iw_cols = [iw[:, IDX_DIM + h:IDX_DIM + h + 1] for h in range(IDX_HEADS)]
    col_iota = lax.broadcasted_iota(I32, (qb, kc), 1)

    def score_body(ci, carry):
        off = pl.multiple_of(ci * kc, kc)
        rel = jnp.maximum(_dot(iq_stack, ik3_ref[:, pl.ds(off, kc)]), 0.0)
        sc = rel[0:qb] * iw_cols[0]
        for h in range(1, IDX_HEADS):
            sc = sc + rel[h * qb:(h + 1) * qb] * iw_cols[h]
        sc = sc + 0.0
        sc = jnp.where(off + col_iota <= q_pos, sc, -jnp.inf)
        bits = lax.bitcast_convert_type(sc, I32)
        key = bits ^ ((bits >> 31) & 0x7FFFFFFF)
        keys_ref[:, pl.ds(off, kc)] = key
        hi_ref[:, pl.ds(off, kc)] = (key >> 16).astype(I16)
        return carry

    lax.fori_loop(0, n_kc, score_body, 0)

    def count(ref, pred, one, zero):
        def body(ci, acc):
            off = pl.multiple_of(ci * kc, kc)
            x = jnp.where(pred(ref[:, pl.ds(off, kc)], off), one, zero)
            part = x[:, 0:128]
            for t in range(1, kc // 128):
                part = part + x[:, t * 128:(t + 1) * 128]
            return acc + part
        acc = lax.fori_loop(0, n_kc, body, jnp.zeros((qb, 128), one.dtype))
        return jnp.sum(acc.astype(F32), axis=1, keepdims=True)

    def count16(ref, pred):
        return count(ref, pred, jnp.int16(1), jnp.int16(0))

    def count32(pred):
        return count(keys_ref, pred, jnp.float32(1), jnp.float32(0))

    def search16(ref, want, known_u=0, n_bits=16):
        def bit_body(i, t_u):
            cand_u = t_u | lax.shift_left(jnp.int32(1), jnp.int32(n_bits - 1) - i)
            cand = (cand_u - 32768).astype(I16)
            cnt = count16(ref, lambda kk_, off: kk_ >= cand)
            return jnp.where(cnt >= want, cand_u, t_u)
        return lax.fori_loop(0, n_bits, bit_body, jnp.full((qb, 1), known_u, I32)) - 32768

    t_hi = search16(hi_ref, float(topk))
    t_hi16 = t_hi.astype(I16)
    want_lo = float(topk) - count16(hi_ref, lambda kk_, off: kk_ > t_hi16)

    def lo_body(ci, carry):
        off = pl.multiple_of(ci * kc, kc)
        low = ((keys_ref[:, pl.ds(off, kc)] & 0xFFFF) - 32768).astype(I16)
        lo_ref[:, pl.ds(off, kc)] = jnp.where(hi_ref[:, pl.ds(off, kc)] == t_hi16, low, jnp.int16(-32768))
        return carry

    lax.fori_loop(0, n_kc, lo_body, 0)
    t_lo = search16(lo_ref, want_lo)
    tau = lax.shift_left(t_hi, 16) | (t_lo + 32768)
    need = float(topk) - count32(lambda kk_, off: kk_ > tau)

    idx_bits = max(1, int(math.ceil(math.log2(seq))))

    def tie_body(ci, carry):
        off = pl.multiple_of(ci * kc, kc)
        rev = (seq - 1) - (off + col_iota)
        lo_ref[:, pl.ds(off, kc)] = jnp.where(keys_ref[:, pl.ds(off, kc)] == tau, rev, -1).astype(I16)
        return carry

    lax.fori_loop(0, n_kc, tie_body, 0)
    c_star = (seq - 1) - search16(lo_ref, need, known_u=32768, n_bits=idx_bits)

    q = q_ref[...]
    cos_q = jnp.concatenate([cos64_ref[pl.ds(q0, qb), :]] * 2, axis=1)
    sin_q = jnp.concatenate([sin64_ref[pl.ds(q0, qb), :]] * 2, axis=1)
    q = (q * cos_q + _rot_half(q, HEAD_DIM // 2) * sin_q) * (HEAD_DIM ** -0.5)
    q_lhs = []
    for h in range(N_HEADS):
        half = q[:, 128 * (h >> 1):128 * (h >> 1) + 128]
        half = jnp.where((lane >> 6) == (h & 1), half, 0.0)
        q_lhs.append((pltpu.roll(half, 64, 1) if h & 1 else half).astype(BF16))
    neg = -1e30
    q_stack = jnp.concatenate(q_lhs, axis=0)
    m_ref[...] = jnp.full_like(m_ref, neg)
    acc_ref[...] = jnp.zeros_like(acc_ref)

    def attn_body(ci, carry):
        off = pl.multiple_of(ci * kc, kc)
        kk_ = keys_ref[:, pl.ds(off, kc)]
        colp = off + col_iota
        sel = ((kk_ > tau) | ((kk_ == tau) & (colp <= c_star))) & (colp <= q_pos)
        bias = jnp.where(sel, 0.0, neg)
        kx = kx_ref[:, pl.ds(off, kc)]
        vx = vx_ref[pl.ds(off, kc), :]
        s_ref[...] = _dot(q_stack, kx) + jnp.concatenate([bias] * N_HEADS, axis=0)
        m_old = m_ref[...]
        m_new = jnp.maximum(m_old, jnp.max(s_ref[...], axis=-1, keepdims=True))
        for t in range(kc // LANES):
            sl = slice(t * LANES, (t + 1) * LANES)
            p_ref[:, sl] = jnp.exp(s_ref[:, sl] - m_new).astype(BF16)
        acc_ref[...] = jnp.exp(m_old - m_new) * acc_ref[...] + _dot(p_ref[...], vx)
        m_ref[...] = m_new
        return carry

    lax.fori_loop(0, n_kc, attn_body, 0)
    halves = []
    for pair in range(N_HEADS // 2):
        outs = []
        for h in (2 * pair, 2 * pair + 1):
            a = acc_ref[h * qb:(h + 1) * qb]
            outs.append(a / pltpu.roll(a, 64, 1))
        halves.append(jnp.where(lane < HEAD_DIM, outs[0], pltpu.roll(outs[1], 64, 1)))
    o_ref[...] = jnp.concatenate(halves, axis=1)


def _dsa(dsa_p, idx_k_norm, cos32, sin32, cos64, sin64):
    b, s, _ = dsa_p.shape
    qb = DSA_QBLOCK
    kc = min(DSA_KEY_CHUNK, s)
    topk = min(DSA_TOPK_MAX, s // 4)
    assert s % kc == 0 and s <= 2 ** 15, "key positions are searched as packed 16-bit values"
    gpad = jnp.pad(idx_k_norm.astype(F32), (0, 128 - IDX_DIM)).reshape(1, 128)
    full = lambda a: pl.BlockSpec(a.shape, lambda i, j: (0,) * a.ndim)
    return pl.pallas_call(
        functools.partial(_dsa_kernel, seq=s, topk=topk, kc=kc),
        grid=(b, s // qb),
        in_specs=[pl.BlockSpec((None, qb, 256), lambda i, j: (i, j, 0)),
                  pl.BlockSpec((None, qb, 128), lambda i, j: (i, j, 2)),
                  pl.BlockSpec((None, s, 128), lambda i, j: (i, 0, 3)),
                  pl.BlockSpec((None, s, 128), lambda i, j: (i, 0, 4)),
                  full(cos32), full(sin32), full(cos64), full(sin64), full(gpad)],
        out_specs=pl.BlockSpec((None, qb, 256), lambda i, j: (i, j, 0)),
        out_shape=jax.ShapeDtypeStruct((b, s, 256), F32),
        scratch_shapes=[pltpu.VMEM((128, s), BF16), pltpu.VMEM((128, s), BF16), pltpu.VMEM((s, 128), BF16),
                        pltpu.VMEM((qb, s), I32), pltpu.VMEM((qb, s), I16), pltpu.VMEM((qb, s), I16),
                        pltpu.VMEM((N_HEADS * qb, LANES), F32), pltpu.VMEM((N_HEADS * qb, LANES), F32),
                        pltpu.VMEM((N_HEADS * qb, kc), F32), pltpu.VMEM((N_HEADS * qb, kc), BF16)],
        compiler_params=_cparams("parallel", "arbitrary"), name="dsa",
    )(dsa_p, dsa_p, dsa_p, dsa_p, cos32, sin32, cos64, sin64, gpad)


def _mix_cross_kernel(h_ref, o0_ref, o1_ref, o2_ref, o3_ref, wout_ref, g_ref, wq_ref, k_ref, v_ref, wo_ref,
                      out_ref):
    h = h_ref[...]
    for i, o_ref in enumerate((o0_ref, o1_ref, o2_ref, o3_ref)):
        h = h + _dot(o_ref[...].astype(BF16), wout_ref[i * GROUP_WIDTH:(i + 1) * GROUP_WIDTH, :])
    q = _dot(_rms(h, g_ref[...]).astype(BF16), wq_ref[...]) * (X_HEAD_DIM ** -0.5)
    outs = []
    for hd in range(X_HEADS):
        sl = slice(hd * X_HEAD_DIM, (hd + 1) * X_HEAD_DIM)
        s = _dot_nt(q[:, sl].astype(BF16), k_ref[:, sl])
        p = jnp.exp(s - jnp.max(s, axis=-1, keepdims=True))
        p = p / jnp.sum(p, axis=-1, keepdims=True)
        outs.append(_dot(p.astype(BF16), v_ref[:, sl]))
    o = jnp.concatenate(outs, axis=1).astype(BF16)
    out_ref[...] = h + _dot(o, wo_ref[...])


def _mix_cross(h2d, outs, w_out, g_cross, wq, k_mem, v_mem, wo, seq, tm):
    t, d = h2d.shape
    n_mem = k_mem.shape[1]
    per_seq = seq // tm
    tok = lambda w: pl.BlockSpec((tm, w), lambda i: (i, 0))
    const = lambda a: pl.BlockSpec(a.shape, lambda i: (0,) * a.ndim)
    mem = pl.BlockSpec((None, n_mem, d), lambda i: (i // per_seq, 0, 0))
    return pl.pallas_call(
        _mix_cross_kernel,
        grid=(t // tm,),
        in_specs=[tok(d)] + [tok(GROUP_WIDTH)] * 4 + [const(w_out), pl.BlockSpec((1, d), lambda i: (0, 0)),
                                                     const(wq), mem, mem, const(wo)],
        out_specs=tok(d),
        out_shape=jax.ShapeDtypeStruct((t, d), F32),
        compiler_params=_cparams("parallel"), name="mix_cross",
    )(h2d, *outs, w_out, g_cross.reshape(1, d), wq, k_mem, v_mem, wo)


def _ffn_kernel(h_ref, g_ref, wg_ref, wv_ref, cw_ref, cb_ref, wd_ref, gf_ref, out_ref, tail_ref, acc_ref,
                *, per_seq, n_chunks, final_norm):
    tm = h_ref.shape[0]

    @pl.when(pl.program_id(0) % per_seq == 0)
    def _():
        tail_ref[...] = jnp.zeros_like(tail_ref)

    h = h_ref[...]
    xb = _rms(h, g_ref[...]).astype(BF16)
    acc_ref[...] = jnp.zeros_like(acc_ref)

    def body(ci, carry):
        gate = _dot(xb, wg_ref[ci])
        val = _dot(xb, wv_ref[ci])
        tail = tail_ref[ci]
        cw = cw_ref[ci]
        conv = gate * cw[FFN_CONV - 1:FFN_CONV, :] + cb_ref[ci]
        for m in range(1, FFN_CONV):
            conv = conv + _shift_rows(tail, gate, m) * cw[FFN_CONV - 1 - m:FFN_CONV - m, :]
        tail_ref[ci] = gate[tm - SUBLANES:tm, :]
        acc_ref[...] += _dot((_silu(conv) * val).astype(BF16), wd_ref[ci])
        return carry

    lax.fori_loop(0, n_chunks, body, 0, unroll=True)
    out = h + acc_ref[...]
    if final_norm:
        out = _rms(out, gf_ref[...])
    out_ref[...] = out


def _ffn(h2d, g_ffn, w_up, conv_w, conv_b, w_down, g_final, seq, tm, final_norm):
    t, d = h2d.shape
    fc = FF_CHUNK
    nck = D_FF // fc
    chunked = lambda w: w.reshape(w.shape[0], nck, fc).transpose(1, 0, 2)
    wg = chunked(w_up[:, :D_FF]).astype(BF16)
    wv = chunked(w_up[:, D_FF:]).astype(BF16)
    cw = chunked(conv_w.astype(F32))
    cb = conv_b.astype(F32).reshape(nck, 1, fc)
    wd = w_down.reshape(nck, fc, d).astype(BF16)
    const = lambda a: pl.BlockSpec(a.shape, lambda i: (0,) * a.ndim)
    row = pl.BlockSpec((1, d), lambda i: (0, 0))
    return pl.pallas_call(
        functools.partial(_ffn_kernel, per_seq=seq // tm, n_chunks=nck, final_norm=final_norm),
        grid=(t // tm,),
        in_specs=[pl.BlockSpec((tm, d), lambda i: (i, 0)), row, const(wg), const(wv), const(cw), const(cb),
                  const(wd), row],
        out_specs=pl.BlockSpec((tm, d), lambda i: (i, 0)),
        out_shape=jax.ShapeDtypeStruct((t, d), F32),
        scratch_shapes=[pltpu.VMEM((nck, SUBLANES, fc), F32), pltpu.VMEM((tm, d), F32)],
        compiler_params=_cparams("arbitrary"), name="conv_glu",
    )(h2d, g_ffn.reshape(1, d), wg, wv, cw, cb, wd, g_final.reshape(1, d))


def _rope_tables(seq, head_dim, width):
    half = head_dim // 2
    inv = ROPE_THETA ** (-jnp.arange(half, dtype=F32) / half)
    ang = jnp.arange(seq, dtype=F32)[:, None] * inv[None, :]
    reps = width // half
    return jnp.tile(jnp.cos(ang), (1, reps)), jnp.tile(jnp.sin(ang), (1, reps))


def _split_w_in(w):
    ret = w[:, 0:768]
    rw = w[:, 768:1792]
    ssm = jnp.pad(w[:, 1792:2820], ((0, 0), (0, 1152 - 1028)))
    d = w[:, 2820:3368]
    dsa = jnp.concatenate([d[:, 0:256], d[:, 384:512], d[:, 256:384], d[:, 512:548],
                           jnp.zeros((w.shape[0], 128 - 36), w.dtype)], axis=1)
    return [t.astype(BF16) for t in (ret, rw, ssm, dsa)]


def kernel(x, mem, norm_mix, w_in, rwkv_mu, rwkv_w0, rwkv_w2, rwkv_a0, rwkv_a2, rwkv_g2, rwkv_k_k, rwkv_k_a, rwkv_r_k, rwkv_ln_w, rwkv_ln_b, ssm_conv_w, ssm_conv_b, ssm_dt_bias, ssm_a_log, ssm_d, ssm_norm, idx_k_norm, w_out, norm_cross, norm_mem, wq_x, wk_x, wv_x, wo_x, norm_ffn, w_up, ffn_conv_w, ffn_conv_b, w_down, norm_final):
    b, s, d = x.shape
    n_mem = mem.shape[1]
    depth = w_in.shape[0]
    t = b * s
    tm = min(512, s)
    cos32, sin32 = _rope_tables(s, RET_QK_DIM, 128)
    cos64, sin64 = _rope_tables(s, HEAD_DIM, 128)
    mem2d = mem.reshape(b * n_mem, d)
    h = x.reshape(t, d)
    for l in range(depth):
        ret_p, rw_p, ssm_p, dsa_p = _norm_matmul(h, norm_mix[l], _split_w_in(w_in[l]), [F32] * 4, tm)
        o_ret = _retention(ret_p.reshape(b, s, -1), cos32, sin32)
        o_rw = _rwkv(rw_p.reshape(b, s, -1), rwkv_mu[l], rwkv_w0[l], rwkv_w2[l], rwkv_a0[l], rwkv_a2[l],
                     rwkv_g2[l], rwkv_k_k[l], rwkv_k_a[l], rwkv_r_k[l], rwkv_ln_w[l], rwkv_ln_b[l])
        o_ssm = _ssd(ssm_p.reshape(b, s, -1), ssm_conv_w[l], ssm_conv_b[l], ssm_dt_bias[l], ssm_a_log[l],
                     ssm_d[l], ssm_norm[l])
        o_dsa = _dsa(dsa_p.reshape(b, s, -1), idx_k_norm[l], cos32, sin32, cos64, sin64)
        k_mem, v_mem = _norm_matmul(mem2d, norm_mem[l], [wk_x[l].astype(BF16), wv_x[l].astype(BF16)],
                                    [BF16, BF16], min(512, b * n_mem))
        outs = [o.reshape(t, GROUP_WIDTH) for o in (o_ret, o_rw, o_ssm, o_dsa)]
        h = _mix_cross(h, outs, w_out[l].astype(BF16), norm_cross[l], wq_x[l].astype(BF16),
                       k_mem.reshape(b, n_mem, d), v_mem.reshape(b, n_mem, d), wo_x[l].astype(BF16), s, tm)
        h = _ffn(h, norm_ffn[l], w_up[l], ffn_conv_w[l], ffn_conv_b[l], w_down[l], norm_final, s, tm,
                 final_norm=(l == depth - 1))
    return h.reshape(b, s, d)
```

```python
import functools
import math

import jax
import jax.numpy as jnp
import numpy as np
from jax import lax
from jax.experimental import pallas as pl
from jax.experimental.pallas import tpu as pltpu

F32 = jnp.float32
BF16 = jnp.bfloat16
I32 = jnp.int32
I16 = jnp.int16

D_MODEL = 1024
GROUP_WIDTH = 256
HEAD_DIM = 64
N_HEADS = 4
RET_QK_DIM = 32
RW_DECAY_SCALE = math.exp(-0.5)
RW_GN_EPS = 64e-5
SSM_STATE = 128
SSM_CONV = 4
SSM_XBC = 768
IDX_HEADS = 4
IDX_DIM = 32
DSA_TOPK_MAX = 256
DSA_QBLOCK = 256
X_HEADS = 4
X_HEAD_DIM = 256
D_FF = 2816
FFN_CONV = 3
ROPE_THETA = 10000.0
NORM_EPS = 1e-6

VMEM_LIMIT_BYTES = 52 * 1024 * 1024
SUBLANES = 8
LANES = 128

RET_CHUNK = 256
SSD_CHUNK = 128
RW_CHUNK = 64
RW_SEQS_PER_STEP = 2
DSA_KEY_CHUNK = 512
FF_CHUNK = 256


def _cparams(*sem):
    return pltpu.CompilerParams(dimension_semantics=sem, vmem_limit_bytes=VMEM_LIMIT_BYTES)


def _dot(a, b):
    return jnp.dot(a, b, preferred_element_type=F32)


def _dot_nt(a, b):
    return lax.dot_general(a, b, (((1,), (1,)), ((), ())), preferred_element_type=F32)


def _dot_tn(a, b):
    return lax.dot_general(a, b, (((0,), (0,)), ((), ())), preferred_element_type=F32)


def _split(a):
    hi = a.astype(BF16)
    lo = (a - hi.astype(F32)).astype(BF16)
    return hi, lo


def _dot_exact_rhs(a, b_bf16):
    hi, lo = _split(a)
    return _dot(hi, b_bf16) + _dot(lo, b_bf16)


def _dot3(a, b):
    ah, al = _split(a)
    bh, bl = _split(b)
    return _dot(ah, bh) + (_dot(ah, bl) + _dot(al, bh))


def _rms(x, g):
    return x * lax.rsqrt(jnp.mean(x * x, axis=-1, keepdims=True) + NORM_EPS) * g


def _silu(x):
    return x * jax.nn.sigmoid(x)


def _block_diag_ones(n, group_shift):
    r = lax.broadcasted_iota(I32, (n, n), 0) >> group_shift
    c = lax.broadcasted_iota(I32, (n, n), 1) >> group_shift
    return jnp.where(r == c, 1.0, 0.0).astype(BF16)


def _rot_half(x, half):
    n = x.shape[-1]
    lane = lax.broadcasted_iota(I32, x.shape, 1)
    first = (lane & (2 * half - 1)) < half
    return jnp.where(first, -pltpu.roll(x, n - half, 1), pltpu.roll(x, half, 1))


def _shift_rows(tail8, x, m):
    c = x.shape[0]
    xx = jnp.concatenate([tail8, x], axis=0)
    return pltpu.roll(xx, m, 0)[SUBLANES:SUBLANES + c]


def _cumsum_rows(x):
    c = x.shape[0]
    row = lax.broadcasted_iota(I32, x.shape, 0)
    d = 1
    while d < c:
        x = x + jnp.where(row >= d, pltpu.roll(x, d, 0), 0.0)
        d *= 2
    return x


def _head_select(cols, shift, width):
    c = cols[0].shape[0]
    lane_head = lax.broadcasted_iota(I32, (c, width), 1) >> shift
    out = jnp.zeros((c, width), F32)
    for h, col in enumerate(cols):
        out = jnp.where(lane_head == h, col, out)
    return out


def _norm_matmul_kernel(x_ref, g_ref, *refs, n_w):
    w_refs, o_refs = refs[:n_w], refs[n_w:]
    xb = _rms(x_ref[...], g_ref[...]).astype(BF16)
    for w_ref, o_ref in zip(w_refs, o_refs):
        o_ref[...] = _dot(xb, w_ref[...]).astype(o_ref.dtype)


def _norm_matmul(x2d, gain, weights, out_dtypes, tm):
    t, d = x2d.shape
    n_w = len(weights)
    in_specs = [pl.BlockSpec((tm, d), lambda i: (i, 0)), pl.BlockSpec((1, d), lambda i: (0, 0))]
    in_specs += [pl.BlockSpec(w.shape, lambda i: (0, 0)) for w in weights]
    out_specs = [pl.BlockSpec((tm, w.shape[1]), lambda i: (i, 0)) for w in weights]
    out_shape = [jax.ShapeDtypeStruct((t, w.shape[1]), dt) for w, dt in zip(weights, out_dtypes)]
    return pl.pallas_call(
        functools.partial(_norm_matmul_kernel, n_w=n_w),
        grid=(t // tm,), in_specs=in_specs, out_specs=out_specs, out_shape=out_shape,
        compiler_params=_cparams("parallel"), name="norm_matmul",
    )(x2d, gain.reshape(1, d), *weights)


def _ret_kernel(blk_ref, cos_ref, sin_ref, o_ref, state_ref, *, chunk):
    c = chunk

    @pl.when(pl.program_id(1) == 0)
    def _():
        state_ref[...] = jnp.zeros_like(state_ref)

    cos, sin = cos_ref[...], sin_ref[...]
    q = blk_ref[:, 0:128]
    k = blk_ref[:, 128:256]
    v = blk_ref[:, 256:512]
    g = blk_ref[:, 512:768]
    q = q * cos + _rot_half(q, RET_QK_DIM // 2) * sin
    k = (k * cos + _rot_half(k, RET_QK_DIM // 2) * sin) * (RET_QK_DIM ** -0.5)

    log_gamma = [math.log(1.0 - 2.0 ** (-5.0 - h)) for h in range(N_HEADS)]
    lane_k = lax.broadcasted_iota(I32, (1, 128), 1) >> 5
    lg_lane = jnp.zeros((1, 128), F32)
    for h in range(N_HEADS):
        lg_lane = jnp.where(lane_k == h, log_gamma[h], lg_lane)
    t_col = lax.broadcasted_iota(I32, (c, 1), 0).astype(F32)
    qd = q * jnp.exp((t_col + 1.0) * lg_lane)
    kd = k * jnp.exp((float(c - 1) - t_col) * lg_lane)

    ti = lax.broadcasted_iota(I32, (c, c), 0)
    si = lax.broadcasted_iota(I32, (c, c), 1)
    causal = ti >= si
    diff = jnp.where(causal, ti - si, 0).astype(F32)
    lane_qk = lax.broadcasted_iota(I32, (c, 128), 1) >> 5
    lane_v = lax.broadcasted_iota(I32, (c, 256), 1) >> 6
    kb = k.astype(BF16)
    vb = v.astype(BF16)
    y = _dot(qd.astype(BF16), state_ref[...].astype(BF16))
    for h in range(N_HEADS):
        qm = jnp.where(lane_qk == h, q, 0.0).astype(BF16)
        s = _dot_nt(qm, kb)
        dec = jnp.where(causal, jnp.exp(diff * log_gamma[h]), 0.0)
        yh = _dot((s * dec).astype(BF16), vb)
        y = y + jnp.where(lane_v == h, yh, 0.0)

    row_h = lax.broadcasted_iota(I32, (128, 256), 0) >> 5
    col_h = lax.broadcasted_iota(I32, (128, 256), 1) >> 6
    lg_row = jnp.zeros((128, 256), F32)
    for h in range(N_HEADS):
        lg_row = jnp.where(row_h == h, log_gamma[h], lg_row)
    new = _dot_tn(kd.astype(BF16), vb)
    state_ref[...] = state_ref[...] * jnp.exp(float(c) * lg_row) + jnp.where(row_h == col_h, new, 0.0)

    ms = _dot_exact_rhs(y * y, _block_diag_ones(256, 6)) * (1.0 / HEAD_DIM)
    o_ref[...] = y * lax.rsqrt(ms + NORM_EPS) * _silu(g)


def _retention(ret_p, cos32, sin32):
    b, s, _ = ret_p.shape
    c = min(RET_CHUNK, s)
    return pl.pallas_call(
        functools.partial(_ret_kernel, chunk=c),
        grid=(b, s // c),
        in_specs=[pl.BlockSpec((None, c, 768), lambda i, j: (i, j, 0)),
                  pl.BlockSpec((c, 128), lambda i, j: (j, 0)),
                  pl.BlockSpec((c, 128), lambda i, j: (j, 0))],
        out_specs=pl.BlockSpec((None, c, 256), lambda i, j: (i, j, 0)),
        out_shape=jax.ShapeDtypeStruct((b, s, 256), F32),
        scratch_shapes=[pltpu.VMEM((128, 256), F32)],
        compiler_params=_cparams("parallel", "arbitrary"), name="retention",
    )(ret_p, cos32, sin32)


def _ssd_kernel(blk_ref, cw_ref, cb_ref, dtb_ref, aneg_ref, dsk_ref, nw_ref, o_ref,
                state_ref, tail_ref, *, chunk):
    c = chunk

    @pl.when(pl.program_id(1) == 0)
    def _():
        state_ref[...] = jnp.zeros_like(state_ref)
        tail_ref[...] = jnp.zeros_like(tail_ref)

    z = blk_ref[:, 0:256]
    xbc = blk_ref[:, 256:1024]
    dt_raw = blk_ref[:, 1024:1152]

    tail = tail_ref[...]
    conv = xbc * cw_ref[SSM_CONV - 1:SSM_CONV, :] + cb_ref[...]
    for m in range(1, SSM_CONV):
        conv = conv + _shift_rows(tail, xbc, m) * cw_ref[SSM_CONV - 1 - m:SSM_CONV - m, :]
    tail_ref[...] = xbc[c - SUBLANES:c, :]
    xbc = _silu(conv)
    xs = xbc[:, 0:256]
    bm = xbc[:, 256:512]
    cm = xbc[:, 512:768]

    u = dt_raw + dtb_ref[...]
    dt = jnp.maximum(u, 0.0) + jnp.log1p(jnp.exp(-jnp.abs(u)))
    la = dt * aneg_ref[...]
    cum = _cumsum_rows(la)
    cum_t = cum.T
    cum_last = cum[c - 1:c, :]

    dt_lane = _head_select([dt[:, h:h + 1] for h in range(N_HEADS)], 6, 256)
    cum_lane = _head_select([cum[:, h:h + 1] for h in range(N_HEADS)], 6, 256)
    end_lane = _head_select([jnp.broadcast_to(cum_last[:, h:h + 1], (c, 1)) for h in range(N_HEADS)], 6, 256)
    xdt = xs * dt_lane
    x_end = (xdt * jnp.exp(end_lane - cum_lane)).astype(BF16)
    xdt_b = xdt.astype(BF16)
    e_lane = jnp.exp(cum_lane)
    chunk_decay = jnp.exp(end_lane[0:1, :])

    ti = lax.broadcasted_iota(I32, (c, c), 0)
    si = lax.broadcasted_iota(I32, (c, c), 1)
    causal = ti >= si
    lane_half = lax.broadcasted_iota(I32, (c, 128), 1) >> 6
    ys = []
    for grp in range(2):
        sl = slice(128 * grp, 128 * grp + 128)
        cg = cm[:, sl].astype(BF16)
        bg = bm[:, sl].astype(BF16)
        s = _dot_nt(cg, bg)
        parts = []
        for hh in range(2):
            h = 2 * grp + hh
            seg = jnp.minimum(cum[:, h:h + 1] - cum_t[h:h + 1, :], 0.0)
            dec = jnp.where(causal, jnp.exp(seg), 0.0)
            parts.append(_dot((s * dec).astype(BF16), xdt_b[:, sl]))
        y_g = jnp.where(lane_half == 0, parts[0], parts[1])
        st = state_ref[grp]
        y_g = y_g + e_lane[:, sl] * _dot(cg, st.astype(BF16))
        state_ref[grp] = st * chunk_decay[:, sl] + _dot_tn(bg, x_end[:, sl])
        ys.append(y_g)
    y = jnp.concatenate(ys, axis=1)
    y = (y + xs * dsk_ref[...]) * _silu(z)
    o_ref[...] = _rms(y, nw_ref[...])


def _ssd(ssm_p, conv_w, conv_b, dt_bias, a_log, d_skip, norm_w):
    b, s, _ = ssm_p.shape
    c = min(SSD_CHUNK, s)
    pad4 = lambda v: jnp.pad(v.astype(F32), (0, 128 - N_HEADS)).reshape(1, 128)
    params = [conv_w.astype(F32), conv_b.reshape(1, SSM_XBC), pad4(dt_bias),
              pad4(-jnp.exp(a_log.astype(F32))),
              jnp.repeat(d_skip, HEAD_DIM).reshape(1, 256), norm_w.reshape(1, 256)]
    full = lambda a: pl.BlockSpec(a.shape, lambda i, j: (0,) * a.ndim)
    return pl.pallas_call(
        functools.partial(_ssd_kernel, chunk=c),
        grid=(b, s // c),
        in_specs=[pl.BlockSpec((None, c, 1152), lambda i, j: (i, j, 0))] + [full(p) for p in params],
        out_specs=pl.BlockSpec((None, c, 256), lambda i, j: (i, j, 0)),
        out_shape=jax.ShapeDtypeStruct((b, s, 256), F32),
        scratch_shapes=[pltpu.VMEM((2, 128, 128), F32), pltpu.VMEM((SUBLANES, SSM_XBC), F32)],
        compiler_params=_cparams("parallel", "arbitrary"), name="ssd",
    )(ssm_p, *params)


def _rwkv_chunk(f, tail, m0, prm, c):
    mu, w0, w2, a0, a2, g2, k_k, k_a, r_k, ln_w, ln_b = prm
    n = N_HEADS * c
    prev = _shift_rows(tail, f, 1)
    x = f + (prev - f) * mu
    r = x[:, 0:256]
    k = x[:, 256:512]
    v = x[:, 512:768]
    lora = x[:, 768:896]
    gl = x[:, 896:1024]

    logw = -RW_DECAY_SCALE * jax.nn.sigmoid(w0 + _dot3(jnp.tanh(lora), w2))
    a = jax.nn.sigmoid(a0 + _dot3(lora, a2))
    gate = _dot(jax.nn.sigmoid(gl).astype(BF16), g2)
    bd = _block_diag_ones(256, 6)
    kk = k * k_k
    kk = kk * lax.rsqrt(_dot_exact_rhs(kk * kk, bd) + 1e-12)
    k2 = k * (1.0 + (a - 1.0) * k_a)

    cum = _cumsum_rows(logw)
    cum_end = cum[c - 1:c, :]
    g_inc = jnp.exp(cum)
    g_exc = jnp.exp(cum - logw)
    g_inv = jnp.exp(-cum)
    g_end = jnp.exp(cum_end - cum)

    lane_h = lax.broadcasted_iota(I32, (c, 256), 1) >> 6

    def stack(t):
        return jnp.concatenate([jnp.where(lane_h == h, t, 0.0) for h in range(N_HEADS)], axis=0).astype(BF16)

    a_s = stack(-kk * g_exc)
    b_s = stack(kk * a * g_inv)
    k_s = stack(k2 * g_inv)
    r_s = stack(r * g_inc)
    v_s = stack(v)
    b_end = stack(kk * a * g_end)
    k_end = stack(k2 * g_end)

    row = lax.broadcasted_iota(I32, (n, n), 0)
    col = lax.broadcasted_iota(I32, (n, n), 1)
    strict = row > col
    incl = row >= col
    l_ab = jnp.where(strict, _dot_nt(a_s, b_s), 0.0)
    l_ak = jnp.where(strict, _dot_nt(a_s, k_s), 0.0).astype(BF16)
    p_rb = jnp.where(incl, _dot_nt(r_s, b_s), 0.0).astype(BF16)
    p_rk = jnp.where(incl, _dot_nt(r_s, k_s), 0.0).astype(BF16)

    m0b = m0.astype(BF16)
    u = _dot(a_s, m0b) + _dot(l_ak, v_s)
    p = l_ab.astype(BF16)
    steps = max(1, int(math.ceil(math.log2(c))))
    for i in range(steps):
        u = u + _dot(p, u.astype(BF16))
        if i + 1 < steps:
            p = _dot(p, p).astype(BF16)
    ub = u.astype(BF16)
    y_s = _dot(r_s, m0b) + _dot(p_rb, ub) + _dot(p_rk, v_s)
    y = y_s[0:c]
    for h in range(1, N_HEADS):
        y = y + y_s[h * c:(h + 1) * c]

    ones = jnp.ones((c, 128), BF16)
    lw_hi, lw_lo = _split(logw)
    lw_lo2 = (logw - lw_hi.astype(F32) - lw_lo.astype(F32)).astype(BF16)
    tot = _dot_tn(lw_hi, ones) + _dot_tn(lw_lo, ones) + _dot_tn(lw_lo2, ones)
    g_col = jnp.exp(jnp.concatenate([tot, tot], axis=1))
    m_new = m0 * g_col + _dot_tn(b_end, ub) + _dot_tn(k_end, v_s)

    inv_n = 1.0 / HEAD_DIM
    mean = _dot_exact_rhs(y, bd) * inv_n
    yc = y - mean
    var = _dot_exact_rhs(yc * yc, bd) * inv_n
    yn = yc * lax.rsqrt(var + RW_GN_EPS) * ln_w + ln_b
    bonus = _dot_exact_rhs(r * k2 * r_k, bd) * v
    return (yn + bonus) * gate, m_new


def _rwkv_kernel(blk_ref, mu_ref, w0_ref, w2_ref, a0_ref, a2_ref, g2_ref, kk_ref, ka_ref, rk_ref,
                 lnw_ref, lnb_ref, o_ref, state_ref, prev_ref, *, chunk, n_seq):
    c = chunk

    @pl.when(pl.program_id(1) == 0)
    def _():
        state_ref[...] = jnp.zeros_like(state_ref)
        prev_ref[...] = jnp.zeros_like(prev_ref)

    prm = (mu_ref[...], w0_ref[...], w2_ref[...], a0_ref[...], a2_ref[...], g2_ref[...].astype(BF16),
           kk_ref[...], ka_ref[...], rk_ref[...], lnw_ref[...], lnb_ref[...])
    for i in range(n_seq):
        f = blk_ref[i]
        out, m_new = _rwkv_chunk(f, prev_ref[i], state_ref[i], prm, c)
        prev_ref[i] = f[c - SUBLANES:c, :]
        state_ref[i] = m_new
        o_ref[i] = out


def _rwkv(rw_p, mu, w0, w2, a0, a2, g2, k_k, k_a, r_k, ln_w, ln_b):
    b, s, _ = rw_p.shape
    c = min(RW_CHUNK, s)
    row = lambda v: v.astype(F32).reshape(1, -1)
    w2p = jnp.concatenate([w2, jnp.zeros_like(w2)], axis=0)
    a2p = jnp.concatenate([jnp.zeros_like(a2), a2], axis=0)
    params = [row(mu), row(w0), w2p, row(a0), a2p, g2, row(k_k), row(k_a), row(r_k), row(ln_w), row(ln_b)]
    full = lambda a: pl.BlockSpec(a.shape, lambda i, j: (0,) * a.ndim)
    n_seq = RW_SEQS_PER_STEP if b % RW_SEQS_PER_STEP == 0 else 1
    return pl.pallas_call(
        functools.partial(_rwkv_kernel, chunk=c, n_seq=n_seq),
        grid=(b // n_seq, s // c),
        in_specs=[pl.BlockSpec((n_seq, c, 1024), lambda i, j: (i, j, 0))] + [full(p) for p in params],
        out_specs=pl.BlockSpec((n_seq, c, 256), lambda i, j: (i, j, 0)),
        out_shape=jax.ShapeDtypeStruct((b, s, 256), F32),
        scratch_shapes=[pltpu.VMEM((n_seq, 256, 256), F32), pltpu.VMEM((n_seq, SUBLANES, 1024), F32)],
        compiler_params=_cparams("parallel", "arbitrary"), name="rwkv7",
    )(rw_p, *params)


def _dsa_kernel(q_ref, iq_ref, kv_ref, iki_ref, cos32_ref, sin32_ref, cos64_ref, sin64_ref, g_ref,
                o_ref, ik3_ref, kx_ref, vx_ref, keys_ref, keyt_ref, hi_ref, lo_ref, m_ref, acc_ref, s_ref, p_ref,
                *, seq, topk, kc):
    qb = DSA_QBLOCK
    j = pl.program_id(1)
    n_kc = (j * qb + qb + kc - 1) // kc

    @pl.when(j == 0)
    def _():
        lane = lax.broadcasted_iota(I32, (seq, 128), 1)
        kv = kv_ref[...]
        kr = kv * cos64_ref[...] + _rot_half(kv, HEAD_DIM // 2) * sin64_ref[...]
        kx_ref[...] = jnp.where(lane < HEAD_DIM, kr, 0.0).T.astype(BF16)
        vx_ref[...] = jnp.where(lane < HEAD_DIM, pltpu.roll(kv, 64, 1), 1.0).astype(BF16)
        ik = jnp.where(lane < IDX_DIM, iki_ref[...], 0.0)
        ik = ik * lax.rsqrt(jnp.sum(ik * ik, axis=-1, keepdims=True) * (1.0 / IDX_DIM) + NORM_EPS) * g_ref[...]
        ik = ik * cos32_ref[...] + _rot_half(ik, IDX_DIM // 2) * sin32_ref[...]
        ik = jnp.where(lane < IDX_DIM, ik, 0.0)
        hi, lo = _split(ik)
        hi, lo = hi.astype(F32), lo.astype(F32)
        ik3_ref[...] = (hi + pltpu.roll(hi, 32, 1) + pltpu.roll(lo, 64, 1)).T.astype(BF16)

    q0 = pl.multiple_of(j * qb, qb)
    q_pos = q0 + lax.broadcasted_iota(I32, (qb, 1), 0)
    lane = lax.broadcasted_iota(I32, (qb, 128), 1)

    iq = iq_ref[...]
    iq = iq * cos32_ref[pl.ds(q0, qb), :] + _rot_half(iq, IDX_DIM // 2) * sin32_ref[pl.ds(q0, qb), :]
    iq_hi, iq_lo = _split(iq)
    iq_hi, iq_lo = iq_hi.astype(F32), iq_lo.astype(F32)
    iq_lhs = []
    for h in range(IDX_HEADS):
        mine = (lane >> 5) == h
        a0 = jnp.where(mine, iq_hi, 0.0)
        b0 = jnp.where(mine, iq_lo, 0.0)
        if h:
            a0 = pltpu.roll(a0, 128 - 32 * h, 1)
            b0 = pltpu.roll(b0, 128 - 32 * h, 1)
        iq_lhs.append((a0 + pltpu.roll(b0, 32, 1) + pltpu.roll(a0, 64, 1)).astype(BF16))
    iq_stack = jnp.concatenate(iq_lhs, axis=0)
    iw = iki_ref[pl.ds(q0, qb), :] * (IDX_HEADS ** -0.5 * IDX_DIM ** -0.5)
    iw_cols = [iw[:, IDX_DIM + h:IDX_DIM + h + 1] for h in range(IDX_HEADS)]
    col_iota = lax.broadcasted_iota(I32, (qb, kc), 1)

    def score_body(ci, carry):
        off = pl.multiple_of(ci * kc, kc)
        rel = jnp.maximum(_dot(iq_stack, ik3_ref[:, pl.ds(off, kc)]), 0.0)
        sc = rel[0:qb] * iw_cols[0]
        for h in range(1, IDX_HEADS):
            sc = sc + rel[h * qb:(h + 1) * qb] * iw_cols[h]
        sc = sc + 0.0
        sc = jnp.where(off + col_iota <= q_pos, sc, -jnp.inf)
        bits = lax.bitcast_convert_type(sc, I32)
        key = bits ^ ((bits >> 31) & 0x7FFFFFFF)
        keys_ref[:, pl.ds(off, kc)] = key
        key_t = key.T
        keyt_ref[pl.ds(off, kc), :] = key_t
        hi_ref[pl.ds(off, kc), :] = (key_t >> 16).astype(I16)
        return carry

    lax.fori_loop(0, n_kc, score_body, 0)

    def count(ref, pred, one, zero, rows):
        def body(ci, acc):
            off = pl.multiple_of(ci * kc, kc)
            x = jnp.where(pred(ref[pl.ds(off, kc), :], off), one, zero)
            parts = [x[r * rows:(r + 1) * rows] for r in range(kc // rows)]
            while len(parts) > 1:
                parts = [parts[i] + parts[i + 1] for i in range(0, len(parts), 2)]
            return acc + parts[0]
        acc = lax.fori_loop(0, n_kc, body, jnp.zeros((rows, qb), one.dtype))
        return jnp.sum(acc.astype(F32), axis=0, keepdims=True)

    def count16(ref, pred):
        return count(ref, pred, jnp.int16(1), jnp.int16(0), 2 * SUBLANES)

    def count32(pred):
        return count(keyt_ref, pred, jnp.float32(1), jnp.float32(0), SUBLANES)

    def search16(ref, want, known_u=0, n_bits=16):
        def bit_body(i, t_u):
            cand_u = t_u | lax.shift_left(jnp.int32(1), jnp.int32(n_bits - 1) - i)
            cand = (cand_u - 32768).astype(I16)
            cnt = count16(ref, lambda kk_, off: kk_ >= cand)
            return jnp.where(cnt >= want, cand_u, t_u)
        return lax.fori_loop(0, n_bits, bit_body, jnp.full((1, qb), known_u, I32)) - 32768

    t_hi = search16(hi_ref, float(topk))
    t_hi16 = t_hi.astype(I16)
    want_lo = float(topk) - count16(hi_ref, lambda kk_, off: kk_ > t_hi16)

    def lo_body(ci, carry):
        off = pl.multiple_of(ci * kc, kc)
        low = ((keyt_ref[pl.ds(off, kc), :] & 0xFFFF) - 32768).astype(I16)
        lo_ref[pl.ds(off, kc), :] = jnp.where(hi_ref[pl.ds(off, kc), :] == t_hi16, low, jnp.int16(-32768))
        return carry

    lax.fori_loop(0, n_kc, lo_body, 0)
    t_lo = search16(lo_ref, want_lo)
    tau_row = lax.shift_left(t_hi, 16) | (t_lo + 32768)
    need = float(topk) - count32(lambda kk_, off: kk_ > tau_row)

    idx_bits = max(1, int(math.ceil(math.log2(seq))))
    row_iota = lax.broadcasted_iota(I32, (kc, qb), 0)

    def tie_body(ci, carry):
        off = pl.multiple_of(ci * kc, kc)
        rev = (seq - 1) - (off + row_iota)
        lo_ref[pl.ds(off, kc), :] = jnp.where(keyt_ref[pl.ds(off, kc), :] == tau_row, rev, -1).astype(I16)
        return carry

    lax.fori_loop(0, n_kc, tie_body, 0)
    c_star_row = (seq - 1) - search16(lo_ref, need, known_u=32768, n_bits=idx_bits)

    def to_column(row):
        as_f32 = lax.bitcast_convert_type(jnp.broadcast_to(row, (LANES, qb)), F32)
        return lax.bitcast_convert_type(as_f32.T, I32)[:, 0:1]

    tau = to_column(tau_row)
    c_star = to_column(c_star_row)

    q = q_ref[...]
    cos_q = jnp.concatenate([cos64_ref[pl.ds(q0, qb), :]] * 2, axis=1)
    sin_q = jnp.concatenate([sin64_ref[pl.ds(q0, qb), :]] * 2, axis=1)
    q = (q * cos_q + _rot_half(q, HEAD_DIM // 2) * sin_q) * (HEAD_DIM ** -0.5)
    q_lhs = []
    for h in range(N_HEADS):
        half = q[:, 128 * (h >> 1):128 * (h >> 1) + 128]
        half = jnp.where((lane >> 6) == (h & 1), half, 0.0)
        q_lhs.append((pltpu.roll(half, 64, 1) if h & 1 else half).astype(BF16))
    neg = -1e30
    q_stack = jnp.concatenate(q_lhs, axis=0)
    m_ref[...] = jnp.full_like(m_ref, neg)
    acc_ref[...] = jnp.zeros_like(acc_ref)

    def attn_body(ci, carry):
        off = pl.multiple_of(ci * kc, kc)
        kk_ = keys_ref[:, pl.ds(off, kc)]
        colp = off + col_iota
        sel = ((kk_ > tau) | ((kk_ == tau) & (colp <= c_star))) & (colp <= q_pos)
        bias = jnp.where(sel, 0.0, neg)
        kx = kx_ref[:, pl.ds(off, kc)]
        vx = vx_ref[pl.ds(off, kc), :]
        s_ref[...] = _dot(q_stack, kx) + jnp.concatenate([bias] * N_HEADS, axis=0)
        m_old = m_ref[...]
        m_new = jnp.maximum(m_old, jnp.max(s_ref[...], axis=-1, keepdims=True))
        for t in range(kc // LANES):
            sl = slice(t * LANES, (t + 1) * LANES)
            p_ref[:, sl] = jnp.exp(s_ref[:, sl] - m_new).astype(BF16)
        acc_ref[...] = jnp.exp(m_old - m_new) * acc_ref[...] + _dot(p_ref[...], vx)
        m_ref[...] = m_new
        return carry

    lax.fori_loop(0, n_kc, attn_body, 0)
    halves = []
    for pair in range(N_HEADS // 2):
        outs = []
        for h in (2 * pair, 2 * pair + 1):
            a = acc_ref[h * qb:(h + 1) * qb]
            outs.append(a / pltpu.roll(a, 64, 1))
        halves.append(jnp.where(lane < HEAD_DIM, outs[0], pltpu.roll(outs[1], 64, 1)))
    o_ref[...] = jnp.concatenate(halves, axis=1)


def _dsa(dsa_p, idx_k_norm, cos32, sin32, cos64, sin64):
    b, s, _ = dsa_p.shape
    qb = DSA_QBLOCK
    kc = min(DSA_KEY_CHUNK, s)
    topk = min(DSA_TOPK_MAX, s // 4)
    assert s % kc == 0 and s <= 2 ** 15, "key positions are searched as packed 16-bit values"
    gpad = jnp.pad(idx_k_norm.astype(F32), (0, 128 - IDX_DIM)).reshape(1, 128)
    full = lambda a: pl.BlockSpec(a.shape, lambda i, j: (0,) * a.ndim)
    return pl.pallas_call(
        functools.partial(_dsa_kernel, seq=s, topk=topk, kc=kc),
        grid=(b, s // qb),
        in_specs=[pl.BlockSpec((None, qb, 256), lambda i, j: (i, j, 0)),
                  pl.BlockSpec((None, qb, 128), lambda i, j: (i, j, 2)),
                  pl.BlockSpec((None, s, 128), lambda i, j: (i, 0, 3)),
                  pl.BlockSpec((None, s, 128), lambda i, j: (i, 0, 4)),
                  full(cos32), full(sin32), full(cos64), full(sin64), full(gpad)],
        out_specs=pl.BlockSpec((None, qb, 256), lambda i, j: (i, j, 0)),
        out_shape=jax.ShapeDtypeStruct((b, s, 256), F32),
        scratch_shapes=[pltpu.VMEM((128, s), BF16), pltpu.VMEM((128, s), BF16), pltpu.VMEM((s, 128), BF16),
                        pltpu.VMEM((qb, s), I32), pltpu.VMEM((s, qb), I32),
                        pltpu.VMEM((s, qb), I16), pltpu.VMEM((s, qb), I16),
                        pltpu.VMEM((N_HEADS * qb, LANES), F32), pltpu.VMEM((N_HEADS * qb, LANES), F32),
                        pltpu.VMEM((N_HEADS * qb, kc), F32), pltpu.VMEM((N_HEADS * qb, kc), BF16)],
        compiler_params=_cparams("parallel", "arbitrary"), name="dsa",
    )(dsa_p, dsa_p, dsa_p, dsa_p, cos32, sin32, cos64, sin64, gpad)


def _mix_cross_kernel(h_ref, o0_ref, o1_ref, o2_ref, o3_ref, wout_ref, g_ref, wq_ref, k_ref, v_ref, wo_ref,
                      out_ref):
    h = h_ref[...]
    for i, o_ref in enumerate((o0_ref, o1_ref, o2_ref, o3_ref)):
        h = h + _dot(o_ref[...].astype(BF16), wout_ref[i * GROUP_WIDTH:(i + 1) * GROUP_WIDTH, :])
    q = _dot(_rms(h, g_ref[...]).astype(BF16), wq_ref[...]) * (X_HEAD_DIM ** -0.5)
    outs = []
    for hd in range(X_HEADS):
        sl = slice(hd * X_HEAD_DIM, (hd + 1) * X_HEAD_DIM)
        s = _dot_nt(q[:, sl].astype(BF16), k_ref[:, sl])
        p = jnp.exp(s - jnp.max(s, axis=-1, keepdims=True))
        p = p / jnp.sum(p, axis=-1, keepdims=True)
        outs.append(_dot(p.astype(BF16), v_ref[:, sl]))
    o = jnp.concatenate(outs, axis=1).astype(BF16)
    out_ref[...] = h + _dot(o, wo_ref[...])


def _mix_cross(h2d, outs, w_out, g_cross, wq, k_mem, v_mem, wo, seq, tm):
    t, d = h2d.shape
    n_mem = k_mem.shape[1]
    per_seq = seq // tm
    tok = lambda w: pl.BlockSpec((tm, w), lambda i: (i, 0))
    const = lambda a: pl.BlockSpec(a.shape, lambda i: (0,) * a.ndim)
    mem = pl.BlockSpec((None, n_mem, d), lambda i: (i // per_seq, 0, 0))
    return pl.pallas_call(
        _mix_cross_kernel,
        grid=(t // tm,),
        in_specs=[tok(d)] + [tok(GROUP_WIDTH)] * 4 + [const(w_out), pl.BlockSpec((1, d), lambda i: (0, 0)),
                                                     const(wq), mem, mem, const(wo)],
        out_specs=tok(d),
        out_shape=jax.ShapeDtypeStruct((t, d), F32),
        compiler_params=_cparams("parallel"), name="mix_cross",
    )(h2d, *outs, w_out, g_cross.reshape(1, d), wq, k_mem, v_mem, wo)


def _ffn_kernel(h_ref, g_ref, wg_ref, wv_ref, cw_ref, cb_ref, wd_ref, gf_ref, out_ref, tail_ref, acc_ref,
                *, per_seq, n_chunks, final_norm):
    tm = h_ref.shape[0]

    @pl.when(pl.program_id(0) % per_seq == 0)
    def _():
        tail_ref[...] = jnp.zeros_like(tail_ref)

    h = h_ref[...]
    xb = _rms(h, g_ref[...]).astype(BF16)
    acc_ref[...] = jnp.zeros_like(acc_ref)

    def body(ci, carry):
        gate = _dot(xb, wg_ref[ci])
        val = _dot(xb, wv_ref[ci])
        tail = tail_ref[ci]
        cw = cw_ref[ci]
        conv = gate * cw[FFN_CONV - 1:FFN_CONV, :] + cb_ref[ci]
        for m in range(1, FFN_CONV):
            conv = conv + _shift_rows(tail, gate, m) * cw[FFN_CONV - 1 - m:FFN_CONV - m, :]
        tail_ref[ci] = gate[tm - SUBLANES:tm, :]
        acc_ref[...] += _dot((_silu(conv) * val).astype(BF16), wd_ref[ci])
        return carry

    lax.fori_loop(0, n_chunks, body, 0, unroll=True)
    out = h + acc_ref[...]
    if final_norm:
        out = _rms(out, gf_ref[...])
    out_ref[...] = out


def _ffn(h2d, g_ffn, w_up, conv_w, conv_b, w_down, g_final, seq, tm, final_norm):
    t, d = h2d.shape
    fc = FF_CHUNK
    nck = D_FF // fc
    chunked = lambda w: w.reshape(w.shape[0], nck, fc).transpose(1, 0, 2)
    wg = chunked(w_up[:, :D_FF]).astype(BF16)
    wv = chunked(w_up[:, D_FF:]).astype(BF16)
    cw = chunked(conv_w.astype(F32))
    cb = conv_b.astype(F32).reshape(nck, 1, fc)
    wd = w_down.reshape(nck, fc, d).astype(BF16)
    const = lambda a: pl.BlockSpec(a.shape, lambda i: (0,) * a.ndim)
    row = pl.BlockSpec((1, d), lambda i: (0, 0))
    return pl.pallas_call(
        functools.partial(_ffn_kernel, per_seq=seq // tm, n_chunks=nck, final_norm=final_norm),
        grid=(t // tm,),
        in_specs=[pl.BlockSpec((tm, d), lambda i: (i, 0)), row, const(wg), const(wv), const(cw), const(cb),
                  const(wd), row],
        out_specs=pl.BlockSpec((tm, d), lambda i: (i, 0)),
        out_shape=jax.ShapeDtypeStruct((t, d), F32),
        scratch_shapes=[pltpu.VMEM((nck, SUBLANES, fc), F32), pltpu.VMEM((tm, d), F32)],
        compiler_params=_cparams("arbitrary"), name="conv_glu",
    )(h2d, g_ffn.reshape(1, d), wg, wv, cw, cb, wd, g_final.reshape(1, d))


def _rope_tables(seq, head_dim, width):
    half = head_dim // 2
    inv = ROPE_THETA ** (-jnp.arange(half, dtype=F32) / half)
    ang = jnp.arange(seq, dtype=F32)[:, None] * inv[None, :]
    reps = width // half
    return jnp.tile(jnp.cos(ang), (1, reps)), jnp.tile(jnp.sin(ang), (1, reps))


def _split_w_in(w):
    ret = w[:, 0:768]
    rw = w[:, 768:1792]
    ssm = jnp.pad(w[:, 1792:2820], ((0, 0), (0, 1152 - 1028)))
    d = w[:, 2820:3368]
    dsa = jnp.concatenate([d[:, 0:256], d[:, 384:512], d[:, 256:384], d[:, 512:548],
                           jnp.zeros((w.shape[0], 128 - 36), w.dtype)], axis=1)
    return [t.astype(BF16) for t in (ret, rw, ssm, dsa)]


def kernel(x, mem, norm_mix, w_in, rwkv_mu, rwkv_w0, rwkv_w2, rwkv_a0, rwkv_a2, rwkv_g2, rwkv_k_k, rwkv_k_a, rwkv_r_k, rwkv_ln_w, rwkv_ln_b, ssm_conv_w, ssm_conv_b, ssm_dt_bias, ssm_a_log, ssm_d, ssm_norm, idx_k_norm, w_out, norm_cross, norm_mem, wq_x, wk_x, wv_x, wo_x, norm_ffn, w_up, ffn_conv_w, ffn_conv_b, w_down, norm_final):
    b, s, d = x.shape
    n_mem = mem.shape[1]
    depth = w_in.shape[0]
    t = b * s
    tm = min(512, s)
    cos32, sin32 = _rope_tables(s, RET_QK_DIM, 128)
    cos64, sin64 = _rope_tables(s, HEAD_DIM, 128)
    mem2d = mem.reshape(b * n_mem, d)
    h = x.reshape(t, d)
    for l in range(depth):
        ret_p, rw_p, ssm_p, dsa_p = _norm_matmul(h, norm_mix[l], _split_w_in(w_in[l]), [F32] * 4, tm)
        o_ret = _retention(ret_p.reshape(b, s, -1), cos32, sin32)
        o_rw = _rwkv(rw_p.reshape(b, s, -1), rwkv_mu[l], rwkv_w0[l], rwkv_w2[l], rwkv_a0[l], rwkv_a2[l],
                     rwkv_g2[l], rwkv_k_k[l], rwkv_k_a[l], rwkv_r_k[l], rwkv_ln_w[l], rwkv_ln_b[l])
        o_ssm = _ssd(ssm_p.reshape(b, s, -1), ssm_conv_w[l], ssm_conv_b[l], ssm_dt_bias[l], ssm_a_log[l],
                     ssm_d[l], ssm_norm[l])
        o_dsa = _dsa(dsa_p.reshape(b, s, -1), idx_k_norm[l], cos32, sin32, cos64, sin64)
        k_mem, v_mem = _norm_matmul(mem2d, norm_mem[l], [wk_x[l].astype(BF16), wv_x[l].astype(BF16)],
                                    [BF16, BF16], min(512, b * n_mem))
        outs = [o.reshape(t, GROUP_WIDTH) for o in (o_ret, o_rw, o_ssm, o_dsa)]
        h = _mix_cross(h, outs, w_out[l].astype(BF16), norm_cross[l], wq_x[l].astype(BF16),
                       k_mem.reshape(b, n_mem, d), v_mem.reshape(b, n_mem, d), wo_x[l].astype(BF16), s, tm)
        h = _ffn(h, norm_ffn[l], w_up[l], ffn_conv_w[l], ffn_conv_b[l], w_down[l], norm_final, s, tm,
                 final_norm=(l == depth - 1))
    return h.reshape(b, s, d)
```

```python
import functools
import math

import jax
import jax.numpy as jnp
import numpy as np
from jax import lax
from jax.experimental import pallas as pl
from jax.experimental.pallas import tpu as pltpu

F32 = jnp.float32
BF16 = jnp.bfloat16
I32 = jnp.int32
I16 = jnp.int16

D_MODEL = 1024
GROUP_WIDTH = 256
HEAD_DIM = 64
N_HEADS = 4
RET_QK_DIM = 32
RW_DECAY_SCALE = math.exp(-0.5)
RW_GN_EPS = 64e-5
SSM_STATE = 128
SSM_CONV = 4
SSM_XBC = 768
IDX_HEADS = 4
IDX_DIM = 32
DSA_TOPK_MAX = 256
DSA_QBLOCK = 256
X_HEADS = 4
X_HEAD_DIM = 256
D_FF = 2816
FFN_CONV = 3
ROPE_THETA = 10000.0
NORM_EPS = 1e-6

VMEM_LIMIT_BYTES = 52 * 1024 * 1024
SUBLANES = 8
LANES = 128

RET_CHUNK = 256
SSD_CHUNK = 128
RW_CHUNK = 64
RW_SEQS_PER_STEP = 4
DSA_KEY_CHUNK = 512
FF_CHUNK = 256


def _cparams(*sem):
    return pltpu.CompilerParams(dimension_semantics=sem, vmem_limit_bytes=VMEM_LIMIT_BYTES)


def _dot(a, b):
    return jnp.dot(a, b, preferred_element_type=F32)


def _dot_nt(a, b):
    return lax.dot_general(a, b, (((1,), (1,)), ((), ())), preferred_element_type=F32)


def _dot_tn(a, b):
    return lax.dot_general(a, b, (((0,), (0,)), ((), ())), preferred_element_type=F32)


def _split(a):
    hi = a.astype(BF16)
    lo = (a - hi.astype(F32)).astype(BF16)
    return hi, lo


def _dot_exact_rhs(a, b_bf16):
    hi, lo = _split(a)
    return _dot(hi, b_bf16) + _dot(lo, b_bf16)


def _dot3(a, b):
    ah, al = _split(a)
    bh, bl = _split(b)
    return _dot(ah, bh) + (_dot(ah, bl) + _dot(al, bh))


def _rms(x, g):
    return x * lax.rsqrt(jnp.mean(x * x, axis=-1, keepdims=True) + NORM_EPS) * g


def _silu(x):
    return x * jax.nn.sigmoid(x)


def _block_diag_ones(n, group_shift):
    r = lax.broadcasted_iota(I32, (n, n), 0) >> group_shift
    c = lax.broadcasted_iota(I32, (n, n), 1) >> group_shift
    return jnp.where(r == c, 1.0, 0.0).astype(BF16)


def _rot_half(x, half):
    n = x.shape[-1]
    lane = lax.broadcasted_iota(I32, x.shape, 1)
    first = (lane & (2 * half - 1)) < half
    return jnp.where(first, -pltpu.roll(x, n - half, 1), pltpu.roll(x, half, 1))


def _shift_rows(tail8, x, m):
    c = x.shape[0]
    xx = jnp.concatenate([tail8, x], axis=0)
    return pltpu.roll(xx, m, 0)[SUBLANES:SUBLANES + c]


def _cumsum_rows(x):
    c = x.shape[0]
    row = lax.broadcasted_iota(I32, x.shape, 0)
    d = 1
    while d < c:
        x = x + jnp.where(row >= d, pltpu.roll(x, d, 0), 0.0)
        d *= 2
    return x


def _head_select(cols, shift, width):
    c = cols[0].shape[0]
    lane_head = lax.broadcasted_iota(I32, (c, width), 1) >> shift
    out = jnp.zeros((c, width), F32)
    for h, col in enumerate(cols):
        out = jnp.where(lane_head == h, col, out)
    return out


def _norm_matmul_kernel(x_ref, g_ref, *refs, n_w):
    w_refs, o_refs = refs[:n_w], refs[n_w:]
    xb = _rms(x_ref[...], g_ref[...]).astype(BF16)
    for w_ref, o_ref in zip(w_refs, o_refs):
        o_ref[...] = _dot(xb, w_ref[...]).astype(o_ref.dtype)


def _norm_matmul(x2d, gain, weights, out_dtypes, tm):
    t, d = x2d.shape
    n_w = len(weights)
    in_specs = [pl.BlockSpec((tm, d), lambda i: (i, 0)), pl.BlockSpec((1, d), lambda i: (0, 0))]
    in_specs += [pl.BlockSpec(w.shape, lambda i: (0, 0)) for w in weights]
    out_specs = [pl.BlockSpec((tm, w.shape[1]), lambda i: (i, 0)) for w in weights]
    out_shape = [jax.ShapeDtypeStruct((t, w.shape[1]), dt) for w, dt in zip(weights, out_dtypes)]
    return pl.pallas_call(
        functools.partial(_norm_matmul_kernel, n_w=n_w),
        grid=(t // tm,), in_specs=in_specs, out_specs=out_specs, out_shape=out_shape,
        compiler_params=_cparams("parallel"), name="norm_matmul",
    )(x2d, gain.reshape(1, d), *weights)


def _ret_kernel(blk_ref, cos_ref, sin_ref, o_ref, state_ref, *, chunk):
    c = chunk

    @pl.when(pl.program_id(1) == 0)
    def _():
        state_ref[...] = jnp.zeros_like(state_ref)

    cos, sin = cos_ref[...], sin_ref[...]
    q = blk_ref[:, 0:128]
    k = blk_ref[:, 128:256]
    v = blk_ref[:, 256:512]
    g = blk_ref[:, 512:768]
    q = q * cos + _rot_half(q, RET_QK_DIM // 2) * sin
    k = (k * cos + _rot_half(k, RET_QK_DIM // 2) * sin) * (RET_QK_DIM ** -0.5)

    log_gamma = [math.log(1.0 - 2.0 ** (-5.0 - h)) for h in range(N_HEADS)]
    lane_k = lax.broadcasted_iota(I32, (1, 128), 1) >> 5
    lg_lane = jnp.zeros((1, 128), F32)
    for h in range(N_HEADS):
        lg_lane = jnp.where(lane_k == h, log_gamma[h], lg_lane)
    t_col = lax.broadcasted_iota(I32, (c, 1), 0).astype(F32)
    qd = q * jnp.exp((t_col + 1.0) * lg_lane)
    kd = k * jnp.exp((float(c - 1) - t_col) * lg_lane)

    ti = lax.broadcasted_iota(I32, (c, c), 0)
    si = lax.broadcasted_iota(I32, (c, c), 1)
    causal = ti >= si
    diff = jnp.where(causal, ti - si, 0).astype(F32)
    lane_qk = lax.broadcasted_iota(I32, (c, 128), 1) >> 5
    lane_v = lax.broadcasted_iota(I32, (c, 256), 1) >> 6
    kb = k.astype(BF16)
    vb = v.astype(BF16)
    y = _dot(qd.astype(BF16), state_ref[...].astype(BF16))
    for h in range(N_HEADS):
        qm = jnp.where(lane_qk == h, q, 0.0).astype(BF16)
        s = _dot_nt(qm, kb)
        dec = jnp.where(causal, jnp.exp(diff * log_gamma[h]), 0.0)
        yh = _dot((s * dec).astype(BF16), vb)
        y = y + jnp.where(lane_v == h, yh, 0.0)

    row_h = lax.broadcasted_iota(I32, (128, 256), 0) >> 5
    col_h = lax.broadcasted_iota(I32, (128, 256), 1) >> 6
    lg_row = jnp.zeros((128, 256), F32)
    for h in range(N_HEADS):
        lg_row = jnp.where(row_h == h, log_gamma[h], lg_row)
    new = _dot_tn(kd.astype(BF16), vb)
    state_ref[...] = state_ref[...] * jnp.exp(float(c) * lg_row) + jnp.where(row_h == col_h, new, 0.0)

    ms = _dot_exact_rhs(y * y, _block_diag_ones(256, 6)) * (1.0 / HEAD_DIM)
    o_ref[...] = y * lax.rsqrt(ms + NORM_EPS) * _silu(g)


def _retention(ret_p, cos32, sin32):
    b, s, _ = ret_p.shape
    c = min(RET_CHUNK, s)
    return pl.pallas_call(
        functools.partial(_ret_kernel, chunk=c),
        grid=(b, s // c),
        in_specs=[pl.BlockSpec((None, c, 768), lambda i, j: (i, j, 0)),
                  pl.BlockSpec((c, 128), lambda i, j: (j, 0)),
                  pl.BlockSpec((c, 128), lambda i, j: (j, 0))],
        out_specs=pl.BlockSpec((None, c, 256), lambda i, j: (i, j, 0)),
        out_shape=jax.ShapeDtypeStruct((b, s, 256), F32),
        scratch_shapes=[pltpu.VMEM((128, 256), F32)],
        compiler_params=_cparams("parallel", "arbitrary"), name="retention",
    )(ret_p, cos32, sin32)


def _ssd_kernel(blk_ref, cw_ref, cb_ref, dtb_ref, aneg_ref, dsk_ref, nw_ref, o_ref,
                state_ref, tail_ref, *, chunk):
    c = chunk

    @pl.when(pl.program_id(1) == 0)
    def _():
        state_ref[...] = jnp.zeros_like(state_ref)
        tail_ref[...] = jnp.zeros_like(tail_ref)

    z = blk_ref[:, 0:256]
    xbc = blk_ref[:, 256:1024]
    dt_raw = blk_ref[:, 1024:1152]

    tail = tail_ref[...]
    conv = xbc * cw_ref[SSM_CONV - 1:SSM_CONV, :] + cb_ref[...]
    for m in range(1, SSM_CONV):
        conv = conv + _shift_rows(tail, xbc, m) * cw_ref[SSM_CONV - 1 - m:SSM_CONV - m, :]
    tail_ref[...] = xbc[c - SUBLANES:c, :]
    xbc = _silu(conv)
    xs = xbc[:, 0:256]
    bm = xbc[:, 256:512]
    cm = xbc[:, 512:768]

    u = dt_raw + dtb_ref[...]
    dt = jnp.maximum(u, 0.0) + jnp.log1p(jnp.exp(-jnp.abs(u)))
    la = dt * aneg_ref[...]
    cum = _cumsum_rows(la)
    cum_t = cum.T
    cum_last = cum[c - 1:c, :]

    dt_lane = _head_select([dt[:, h:h + 1] for h in range(N_HEADS)], 6, 256)
    cum_lane = _head_select([cum[:, h:h + 1] for h in range(N_HEADS)], 6, 256)
    end_lane = _head_select([jnp.broadcast_to(cum_last[:, h:h + 1], (c, 1)) for h in range(N_HEADS)], 6, 256)
    xdt = xs * dt_lane
    x_end = (xdt * jnp.exp(end_lane - cum_lane)).astype(BF16)
    xdt_b = xdt.astype(BF16)
    e_lane = jnp.exp(cum_lane)
    chunk_decay = jnp.exp(end_lane[0:1, :])

    ti = lax.broadcasted_iota(I32, (c, c), 0)
    si = lax.broadcasted_iota(I32, (c, c), 1)
    causal = ti >= si
    lane_half = lax.broadcasted_iota(I32, (c, 128), 1) >> 6
    ys = []
    for grp in range(2):
        sl = slice(128 * grp, 128 * grp + 128)
        cg = cm[:, sl].astype(BF16)
        bg = bm[:, sl].astype(BF16)
        s = _dot_nt(cg, bg)
        parts = []
        for hh in range(2):
            h = 2 * grp + hh
            seg = jnp.minimum(cum[:, h:h + 1] - cum_t[h:h + 1, :], 0.0)
            dec = jnp.where(causal, jnp.exp(seg), 0.0)
            parts.append(_dot((s * dec).astype(BF16), xdt_b[:, sl]))
        y_g = jnp.where(lane_half == 0, parts[0], parts[1])
        st = state_ref[grp]
        y_g = y_g + e_lane[:, sl] * _dot(cg, st.astype(BF16))
        state_ref[grp] = st * chunk_decay[:, sl] + _dot_tn(bg, x_end[:, sl])
        ys.append(y_g)
    y = jnp.concatenate(ys, axis=1)
    y = (y + xs * dsk_ref[...]) * _silu(z)
    o_ref[...] = _rms(y, nw_ref[...])


def _ssd(ssm_p, conv_w, conv_b, dt_bias, a_log, d_skip, norm_w):
    b, s, _ = ssm_p.shape
    c = min(SSD_CHUNK, s)
    pad4 = lambda v: jnp.pad(v.astype(F32), (0, 128 - N_HEADS)).reshape(1, 128)
    params = [conv_w.astype(F32), conv_b.reshape(1, SSM_XBC), pad4(dt_bias),
              pad4(-jnp.exp(a_log.astype(F32))),
              jnp.repeat(d_skip, HEAD_DIM).reshape(1, 256), norm_w.reshape(1, 256)]
    full = lambda a: pl.BlockSpec(a.shape, lambda i, j: (0,) * a.ndim)
    return pl.pallas_call(
        functools.partial(_ssd_kernel, chunk=c),
        grid=(b, s // c),
        in_specs=[pl.BlockSpec((None, c, 1152), lambda i, j: (i, j, 0))] + [full(p) for p in params],
        out_specs=pl.BlockSpec((None, c, 256), lambda i, j: (i, j, 0)),
        out_shape=jax.ShapeDtypeStruct((b, s, 256), F32),
        scratch_shapes=[pltpu.VMEM((2, 128, 128), F32), pltpu.VMEM((SUBLANES, SSM_XBC), F32)],
        compiler_params=_cparams("parallel", "arbitrary"), name="ssd",
    )(ssm_p, *params)


def _rwkv_chunks(fs, tails, m0s, prm, c):
    mu, w0, w2, a0, a2, g2, k_k, k_a, r_k, ln_w, ln_b = prm
    n = N_HEADS * c
    mp = lambda f, *ls: [f(*a) for a in zip(*ls)]
    bd = _block_diag_ones(256, 6)
    lane_h = lax.broadcasted_iota(I32, (c, 256), 1) >> 6
    row = lax.broadcasted_iota(I32, (n, n), 0)
    col = lax.broadcasted_iota(I32, (n, n), 1)
    strict = row > col
    incl = row >= col
    ones = jnp.ones((c, 128), BF16)
    inv_n = 1.0 / HEAD_DIM

    def stack(t):
        return jnp.concatenate([jnp.where(lane_h == h, t, 0.0) for h in range(N_HEADS)], axis=0).astype(BF16)

    xs = mp(lambda f, tail: f + (_shift_rows(tail, f, 1) - f) * mu, fs, tails)
    rs = [x[:, 0:256] for x in xs]
    ks = [x[:, 256:512] for x in xs]
    vs = [x[:, 512:768] for x in xs]
    loras = [x[:, 768:896] for x in xs]
    gls = [x[:, 896:1024] for x in xs]

    logws = mp(lambda lo: -RW_DECAY_SCALE * jax.nn.sigmoid(w0 + _dot3(jnp.tanh(lo), w2)), loras)
    a_s_ = mp(lambda lo: jax.nn.sigmoid(a0 + _dot3(lo, a2)), loras)
    gates = mp(lambda gl: _dot(jax.nn.sigmoid(gl).astype(BF16), g2), gls)
    kks = mp(lambda k: k * k_k, ks)
    kks = mp(lambda kk: kk * lax.rsqrt(_dot_exact_rhs(kk * kk, bd) + 1e-12), kks)
    k2s = mp(lambda k, a: k * (1.0 + (a - 1.0) * k_a), ks, a_s_)

    cums = mp(_cumsum_rows, logws)
    g_incs = mp(jnp.exp, cums)
    g_excs = mp(lambda cum, lw: jnp.exp(cum - lw), cums, logws)
    g_invs = mp(lambda cum: jnp.exp(-cum), cums)
    g_ends = mp(lambda cum: jnp.exp(cum[c - 1:c, :] - cum), cums)

    As = mp(lambda kk, g: stack(-kk * g), kks, g_excs)
    Bs = mp(lambda kk, a, g: stack(kk * a * g), kks, a_s_, g_invs)
    Ks = mp(lambda k2, g: stack(k2 * g), k2s, g_invs)
    Rs = mp(lambda r, g: stack(r * g), rs, g_incs)
    Vs = mp(stack, vs)
    Bends = mp(lambda kk, a, g: stack(kk * a * g), kks, a_s_, g_ends)
    Kends = mp(lambda k2, g: stack(k2 * g), k2s, g_ends)

    ps = mp(lambda A, B: jnp.where(strict, _dot_nt(A, B), 0.0).astype(BF16), As, Bs)
    l_aks = mp(lambda A, K: jnp.where(strict, _dot_nt(A, K), 0.0).astype(BF16), As, Ks)
    p_rbs = mp(lambda R, B: jnp.where(incl, _dot_nt(R, B), 0.0).astype(BF16), Rs, Bs)
    p_rks = mp(lambda R, K: jnp.where(incl, _dot_nt(R, K), 0.0).astype(BF16), Rs, Ks)

    m0bs = mp(lambda m0: m0.astype(BF16), m0s)
    us = mp(lambda A, m0b, l_ak, V: _dot(A, m0b) + _dot(l_ak, V), As, m0bs, l_aks, Vs)
    steps = max(1, int(math.ceil(math.log2(c))))
    for i in range(steps):
        us = mp(lambda u, p: u + _dot(p, u.astype(BF16)), us, ps)
        if i + 1 < steps:
            ps = mp(lambda p: _dot(p, p).astype(BF16), ps)
    ubs = mp(lambda u: u.astype(BF16), us)
    y_ss = mp(lambda R, m0b, p_rb, ub, p_rk, V: _dot(R, m0b) + _dot(p_rb, ub) + _dot(p_rk, V),
              Rs, m0bs, p_rbs, ubs, p_rks, Vs)

    def unstack(y_s):
        y = y_s[0:c]
        for h in range(1, N_HEADS):
            y = y + y_s[h * c:(h + 1) * c]
        return y

    ys = mp(unstack, y_ss)

    def total_decay(logw):
        lw_hi, lw_lo = _split(logw)
        lw_lo2 = (logw - lw_hi.astype(F32) - lw_lo.astype(F32)).astype(BF16)
        tot = _dot_tn(lw_hi, ones) + _dot_tn(lw_lo, ones) + _dot_tn(lw_lo2, ones)
        return jnp.exp(jnp.concatenate([tot, tot], axis=1))

    g_cols = mp(total_decay, logws)
    m_news = mp(lambda m0, g, Be, ub, Ke, V: m0 * g + _dot_tn(Be, ub) + _dot_tn(Ke, V),
                m0s, g_cols, Bends, ubs, Kends, Vs)

    means = mp(lambda y: _dot_exact_rhs(y, bd) * inv_n, ys)
    ycs = mp(lambda y, m: y - m, ys, means)
    vars_ = mp(lambda yc: _dot_exact_rhs(yc * yc, bd) * inv_n, ycs)
    yns = mp(lambda yc, var: yc * lax.rsqrt(var + RW_GN_EPS) * ln_w + ln_b, ycs, vars_)
    bonuses = mp(lambda r, k2, v: _dot_exact_rhs(r * k2 * r_k, bd) * v, rs, k2s, vs)
    outs = mp(lambda yn, bonus, gate: (yn + bonus) * gate, yns, bonuses, gates)
    return outs, m_news


def _rwkv_kernel(blk_ref, mu_ref, w0_ref, w2_ref, a0_ref, a2_ref, g2_ref, kk_ref, ka_ref, rk_ref,
                 lnw_ref, lnb_ref, o_ref, state_ref, prev_ref, *, chunk, n_seq):
    c = chunk

    @pl.when(pl.program_id(1) == 0)
    def _():
        state_ref[...] = jnp.zeros_like(state_ref)
        prev_ref[...] = jnp.zeros_like(prev_ref)

    prm = (mu_ref[...], w0_ref[...], w2_ref[...], a0_ref[...], a2_ref[...], g2_ref[...].astype(BF16),
           kk_ref[...], ka_ref[...], rk_ref[...], lnw_ref[...], lnb_ref[...])
    fs = [blk_ref[i] for i in range(n_seq)]
    outs, m_news = _rwkv_chunks(fs, [prev_ref[i] for i in range(n_seq)],
                                [state_ref[i] for i in range(n_seq)], prm, c)
    for i in range(n_seq):
        prev_ref[i] = fs[i][c - SUBLANES:c, :]
        state_ref[i] = m_news[i]
        o_ref[i] = outs[i]


def _rwkv(rw_p, mu, w0, w2, a0, a2, g2, k_k, k_a, r_k, ln_w, ln_b):
    b, s, _ = rw_p.shape
    c = min(RW_CHUNK, s)
    row = lambda v: v.astype(F32).reshape(1, -1)
    w2p = jnp.concatenate([w2, jnp.zeros_like(w2)], axis=0)
    a2p = jnp.concatenate([jnp.zeros_like(a2), a2], axis=0)
    params = [row(mu), row(w0), w2p, row(a0), a2p, g2, row(k_k), row(k_a), row(r_k), row(ln_w), row(ln_b)]
    full = lambda a: pl.BlockSpec(a.shape, lambda i, j: (0,) * a.ndim)
    n_seq = RW_SEQS_PER_STEP if b % RW_SEQS_PER_STEP == 0 else 1
    return pl.pallas_call(
        functools.partial(_rwkv_kernel, chunk=c, n_seq=n_seq),
        grid=(b // n_seq, s // c),
        in_specs=[pl.BlockSpec((n_seq, c, 1024), lambda i, j: (i, j, 0))] + [full(p) for p in params],
        out_specs=pl.BlockSpec((n_seq, c, 256), lambda i, j: (i, j, 0)),
        out_shape=jax.ShapeDtypeStruct((b, s, 256), F32),
        scratch_shapes=[pltpu.VMEM((n_seq, 256, 256), F32), pltpu.VMEM((n_seq, SUBLANES, 1024), F32)],
        compiler_params=_cparams("parallel", "arbitrary"), name="rwkv7",
    )(rw_p, *params)


def _dsa_kernel(q_ref, iq_ref, kv_ref, iki_ref, cos32_ref, sin32_ref, cos64_ref, sin64_ref, g_ref,
                o_ref, ik3_ref, kx_ref, vx_ref, keys_ref, keyt_ref, hi_ref, lo_ref, m_ref, acc_ref, s_ref, p_ref,
                *, seq, topk, kc):
    qb = DSA_QBLOCK
    j = pl.program_id(1)
    n_kc = (j * qb + qb + kc - 1) // kc

    @pl.when(j == 0)
    def _():
        lane = lax.broadcasted_iota(I32, (seq, 128), 1)
        kv = kv_ref[...]
        kr = kv * cos64_ref[...] + _rot_half(kv, HEAD_DIM // 2) * sin64_ref[...]
        kx_ref[...] = jnp.where(lane < HEAD_DIM, kr, 0.0).T.astype(BF16)
        vx_ref[...] = jnp.where(lane < HEAD_DIM, pltpu.roll(kv, 64, 1), 1.0).astype(BF16)
        ik = jnp.where(lane < IDX_DIM, iki_ref[...], 0.0)
        ik = ik * lax.rsqrt(jnp.sum(ik * ik, axis=-1, keepdims=True) * (1.0 / IDX_DIM) + NORM_EPS) * g_ref[...]
        ik = ik * cos32_ref[...] + _rot_half(ik, IDX_DIM // 2) * sin32_ref[...]
        ik = jnp.where(lane < IDX_DIM, ik, 0.0)
        hi, lo = _split(ik)
        hi, lo = hi.astype(F32), lo.astype(F32)
        ik3_ref[...] = (hi + pltpu.roll(hi, 32, 1) + pltpu.roll(lo, 64, 1)).T.astype(BF16)

    q0 = pl.multiple_of(j * qb, qb)
    q_pos = q0 + lax.broadcasted_iota(I32, (qb, 1), 0)
    lane = lax.broadcasted_iota(I32, (qb, 128), 1)

    iq = iq_ref[...]
    iq = iq * cos32_ref[pl.ds(q0, qb), :] + _rot_half(iq, IDX_DIM // 2) * sin32_ref[pl.ds(q0, qb), :]
    iq_hi, iq_lo = _split(iq)
    iq_hi, iq_lo = iq_hi.astype(F32), iq_lo.astype(F32)
    iq_lhs = []
    for h in range(IDX_HEADS):
        mine = (lane >> 5) == h
        a0 = jnp.where(mine, iq_hi, 0.0)
        b0 = jnp.where(mine, iq_lo, 0.0)
        if h:
            a0 = pltpu.roll(a0, 128 - 32 * h, 1)
            b0 = pltpu.roll(b0, 128 - 32 * h, 1)
        iq_lhs.append((a0 + pltpu.roll(b0, 32, 1) + pltpu.roll(a0, 64, 1)).astype(BF16))
    iq_stack = jnp.concatenate(iq_lhs, axis=0)
    iw = iki_ref[pl.ds(q0, qb), :] * (IDX_HEADS ** -0.5 * IDX_DIM ** -0.5)
    iw_cols = [iw[:, IDX_DIM + h:IDX_DIM + h + 1] for h in range(IDX_HEADS)]
    col_iota = lax.broadcasted_iota(I32, (qb, kc), 1)

    def score_body(ci, carry):
        off = pl.multiple_of(ci * kc, kc)
        rel = jnp.maximum(_dot(iq_stack, ik3_ref[:, pl.ds(off, kc)]), 0.0)
        sc = rel[0:qb] * iw_cols[0]
        for h in range(1, IDX_HEADS):
            sc = sc + rel[h * qb:(h + 1) * qb] * iw_cols[h]
        sc = sc + 0.0
        sc = jnp.where(off + col_iota <= q_pos, sc, -jnp.inf)
        bits = lax.bitcast_convert_type(sc, I32)
        key = bits ^ ((bits >> 31) & 0x7FFFFFFF)
        keys_ref[:, pl.ds(off, kc)] = key
        key_t = key.T
        keyt_ref[pl.ds(off, kc), :] = key_t
        hi_ref[pl.ds(off, kc), :] = (key_t >> 16).astype(I16)
        return carry

    lax.fori_loop(0, n_kc, score_body, 0)

    def count(ref, pred, one, zero, rows):
        def body(ci, acc):
            off = pl.multiple_of(ci * kc, kc)
            x = jnp.where(pred(ref[pl.ds(off, kc), :], off), one, zero)
            parts = [x[r * rows:(r + 1) * rows] for r in range(kc // rows)]
            while len(parts) > 1:
                parts = [parts[i] + parts[i + 1] for i in range(0, len(parts), 2)]
            return acc + parts[0]
        acc = lax.fori_loop(0, n_kc, body, jnp.zeros((rows, qb), one.dtype))
        return jnp.sum(acc.astype(F32), axis=0, keepdims=True)

    def count16(ref, pred):
        return count(ref, pred, jnp.int16(1), jnp.int16(0), 2 * SUBLANES)

    def count32(pred):
        return count(keyt_ref, pred, jnp.float32(1), jnp.float32(0), SUBLANES)

    def search16(ref, want, known_u=0, n_bits=16):
        def bit_body(i, t_u):
            cand_u = t_u | lax.shift_left(jnp.int32(1), jnp.int32(n_bits - 1) - i)
            cand = (cand_u - 32768).astype(I16)
            cnt = count16(ref, lambda kk_, off: kk_ >= cand)
            return jnp.where(cnt >= want, cand_u, t_u)
        return lax.fori_loop(0, n_bits, bit_body, jnp.full((1, qb), known_u, I32)) - 32768

    t_hi = search16(hi_ref, float(topk))
    t_hi16 = t_hi.astype(I16)
    want_lo = float(topk) - count16(hi_ref, lambda kk_, off: kk_ > t_hi16)

    def lo_body(ci, carry):
        off = pl.multiple_of(ci * kc, kc)
        low = ((keyt_ref[pl.ds(off, kc), :] & 0xFFFF) - 32768).astype(I16)
        lo_ref[pl.ds(off, kc), :] = jnp.where(hi_ref[pl.ds(off, kc), :] == t_hi16, low, jnp.int16(-32768))
        return carry

    lax.fori_loop(0, n_kc, lo_body, 0)
    t_lo = search16(lo_ref, want_lo)
    tau_row = lax.shift_left(t_hi, 16) | (t_lo + 32768)
    need = float(topk) - count32(lambda kk_, off: kk_ > tau_row)

    idx_bits = max(1, int(math.ceil(math.log2(seq))))
    row_iota = lax.broadcasted_iota(I32, (kc, qb), 0)

    def tie_body(ci, carry):
        off = pl.multiple_of(ci * kc, kc)
        rev = (seq - 1) - (off + row_iota)
        lo_ref[pl.ds(off, kc), :] = jnp.where(keyt_ref[pl.ds(off, kc), :] == tau_row, rev, -1).astype(I16)
        return carry

    lax.fori_loop(0, n_kc, tie_body, 0)
    c_star_row = (seq - 1) - search16(lo_ref, need, known_u=32768, n_bits=idx_bits)

    def to_column(row):
        as_f32 = lax.bitcast_convert_type(jnp.broadcast_to(row, (LANES, qb)), F32)
        return lax.bitcast_convert_type(as_f32.T, I32)[:, 0:1]

    tau = to_column(tau_row)
    c_star = to_column(c_star_row)

    q = q_ref[...]
    cos_q = jnp.concatenate([cos64_ref[pl.ds(q0, qb), :]] * 2, axis=1)
    sin_q = jnp.concatenate([sin64_ref[pl.ds(q0, qb), :]] * 2, axis=1)
    q = (q * cos_q + _rot_half(q, HEAD_DIM // 2) * sin_q) * (HEAD_DIM ** -0.5)
    q_lhs = []
    for h in range(N_HEADS):
        half = q[:, 128 * (h >> 1):128 * (h >> 1) + 128]
        half = jnp.where((lane >> 6) == (h & 1), half, 0.0)
        q_lhs.append((pltpu.roll(half, 64, 1) if h & 1 else half).astype(BF16))
    neg = -1e30
    q_stack = jnp.concatenate(q_lhs, axis=0)
    m_ref[...] = jnp.full_like(m_ref, neg)
    acc_ref[...] = jnp.zeros_like(acc_ref)

    def attn_body(ci, carry):
        off = pl.multiple_of(ci * kc, kc)
        kk_ = keys_ref[:, pl.ds(off, kc)]
        colp = off + col_iota
        sel = ((kk_ > tau) | ((kk_ == tau) & (colp <= c_star))) & (colp <= q_pos)
        bias = jnp.where(sel, 0.0, neg)
        kx = kx_ref[:, pl.ds(off, kc)]
        vx = vx_ref[pl.ds(off, kc), :]
        s_ref[...] = _dot(q_stack, kx) + jnp.concatenate([bias] * N_HEADS, axis=0)
        m_old = m_ref[...]
        m_new = jnp.maximum(m_old, jnp.max(s_ref[...], axis=-1, keepdims=True))
        for t in range(kc // LANES):
            sl = slice(t * LANES, (t + 1) * LANES)
            p_ref[:, sl] = jnp.exp(s_ref[:, sl] - m_new).astype(BF16)
        acc_ref[...] = jnp.exp(m_old - m_new) * acc_ref[...] + _dot(p_ref[...], vx)
        m_ref[...] = m_new
        return carry

    lax.fori_loop(0, n_kc, attn_body, 0)
    halves = []
    for pair in range(N_HEADS // 2):
        outs = []
        for h in (2 * pair, 2 * pair + 1):
            a = acc_ref[h * qb:(h + 1) * qb]
            outs.append(a / pltpu.roll(a, 64, 1))
        halves.append(jnp.where(lane < HEAD_DIM, outs[0], pltpu.roll(outs[1], 64, 1)))
    o_ref[...] = jnp.concatenate(halves, axis=1)


def _dsa(dsa_p, idx_k_norm, cos32, sin32, cos64, sin64):
    b, s, _ = dsa_p.shape
    qb = DSA_QBLOCK
    kc = min(DSA_KEY_CHUNK, s)
    topk = min(DSA_TOPK_MAX, s // 4)
    assert s % kc == 0 and s <= 2 ** 15, "key positions are searched as packed 16-bit values"
    gpad = jnp.pad(idx_k_norm.astype(F32), (0, 128 - IDX_DIM)).reshape(1, 128)
    full = lambda a: pl.BlockSpec(a.shape, lambda i, j: (0,) * a.ndim)
    return pl.pallas_call(
        functools.partial(_dsa_kernel, seq=s, topk=topk, kc=kc),
        grid=(b, s // qb),
        in_specs=[pl.BlockSpec((None, qb, 256), lambda i, j: (i, j, 0)),
                  pl.BlockSpec((None, qb, 128), lambda i, j: (i, j, 2)),
                  pl.BlockSpec((None, s, 128), lambda i, j: (i, 0, 3)),
                  pl.BlockSpec((None, s, 128), lambda i, j: (i, 0, 4)),
                  full(cos32), full(sin32), full(cos64), full(sin64), full(gpad)],
        out_specs=pl.BlockSpec((None, qb, 256), lambda i, j: (i, j, 0)),
        out_shape=jax.ShapeDtypeStruct((b, s, 256), F32),
        scratch_shapes=[pltpu.VMEM((128, s), BF16), pltpu.VMEM((128, s), BF16), pltpu.VMEM((s, 128), BF16),
                        pltpu.VMEM((qb, s), I32), pltpu.VMEM((s, qb), I32),
                        pltpu.VMEM((s, qb), I16), pltpu.VMEM((s, qb), I16),
                        pltpu.VMEM((N_HEADS * qb, LANES), F32), pltpu.VMEM((N_HEADS * qb, LANES), F32),
                        pltpu.VMEM((N_HEADS * qb, kc), F32), pltpu.VMEM((N_HEADS * qb, kc), BF16)],
        compiler_params=_cparams("parallel", "arbitrary"), name="dsa",
    )(dsa_p, dsa_p, dsa_p, dsa_p, cos32, sin32, cos64, sin64, gpad)


def _mix_cross_kernel(h_ref, o0_ref, o1_ref, o2_ref, o3_ref, wout_ref, g_ref, wq_ref, k_ref, v_ref, wo_ref,
                      out_ref):
    h = h_ref[...]
    for i, o_ref in enumerate((o0_ref, o1_ref, o2_ref, o3_ref)):
        h = h + _dot(o_ref[...].astype(BF16), wout_ref[i * GROUP_WIDTH:(i + 1) * GROUP_WIDTH, :])
    q = _dot(_rms(h, g_ref[...]).astype(BF16), wq_ref[...]) * (X_HEAD_DIM ** -0.5)
    outs = []
    for hd in range(X_HEADS):
        sl = slice(hd * X_HEAD_DIM, (hd + 1) * X_HEAD_DIM)
        s = _dot_nt(q[:, sl].astype(BF16), k_ref[:, sl])
        p = jnp.exp(s - jnp.max(s, axis=-1, keepdims=True))
        p = p / jnp.sum(p, axis=-1, keepdims=True)
        outs.append(_dot(p.astype(BF16), v_ref[:, sl]))
    o = jnp.concatenate(outs, axis=1).astype(BF16)
    out_ref[...] = h + _dot(o, wo_ref[...])


def _mix_cross(h2d, outs, w_out, g_cross, wq, k_mem, v_mem, wo, seq, tm):
    t, d = h2d.shape
    n_mem = k_mem.shape[1]
    per_seq = seq // tm
    tok = lambda w: pl.BlockSpec((tm, w), lambda i: (i, 0))
    const = lambda a: pl.BlockSpec(a.shape, lambda i: (0,) * a.ndim)
    mem = pl.BlockSpec((None, n_mem, d), lambda i: (i // per_seq, 0, 0))
    return pl.pallas_call(
        _mix_cross_kernel,
        grid=(t // tm,),
        in_specs=[tok(d)] + [tok(GROUP_WIDTH)] * 4 + [const(w_out), pl.BlockSpec((1, d), lambda i: (0, 0)),
                                                     const(wq), mem, mem, const(wo)],
        out_specs=tok(d),
        out_shape=jax.ShapeDtypeStruct((t, d), F32),
        compiler_params=_cparams("parallel"), name="mix_cross",
    )(h2d, *outs, w_out, g_cross.reshape(1, d), wq, k_mem, v_mem, wo)


def _ffn_kernel(h_ref, g_ref, wg_ref, wv_ref, cw_ref, cb_ref, wd_ref, gf_ref, out_ref, tail_ref, acc_ref,
                *, per_seq, n_chunks, final_norm):
    tm = h_ref.shape[0]

    @pl.when(pl.program_id(0) % per_seq == 0)
    def _():
        tail_ref[...] = jnp.zeros_like(tail_ref)

    h = h_ref[...]
    xb = _rms(h, g_ref[...]).astype(BF16)
    acc_ref[...] = jnp.zeros_like(acc_ref)

    def body(ci, carry):
        gate = _dot(xb, wg_ref[ci])
        val = _dot(xb, wv_ref[ci])
        tail = tail_ref[ci]
        cw = cw_ref[ci]
        conv = gate * cw[FFN_CONV - 1:FFN_CONV, :] + cb_ref[ci]
        for m in range(1, FFN_CONV):
            conv = conv + _shift_rows(tail, gate, m) * cw[FFN_CONV - 1 - m:FFN_CONV - m, :]
        tail_ref[ci] = gate[tm - SUBLANES:tm, :]
        acc_ref[...] += _dot((_silu(conv) * val).astype(BF16), wd_ref[ci])
        return carry

    lax.fori_loop(0, n_chunks, body, 0, unroll=True)
    out = h + acc_ref[...]
    if final_norm:
        out = _rms(out, gf_ref[...])
    out_ref[...] = out


def _ffn(h2d, g_ffn, w_up, conv_w, conv_b, w_down, g_final, seq, tm, final_norm):
    t, d = h2d.shape
    fc = FF_CHUNK
    nck = D_FF // fc
    chunked = lambda w: w.reshape(w.shape[0], nck, fc).transpose(1, 0, 2)
    wg = chunked(w_up[:, :D_FF]).astype(BF16)
    wv = chunked(w_up[:, D_FF:]).astype(BF16)
    cw = chunked(conv_w.astype(F32))
    cb = conv_b.astype(F32).reshape(nck, 1, fc)
    wd = w_down.reshape(nck, fc, d).astype(BF16)
    const = lambda a: pl.BlockSpec(a.shape, lambda i: (0,) * a.ndim)
    row = pl.BlockSpec((1, d), lambda i: (0, 0))
    return pl.pallas_call(
        functools.partial(_ffn_kernel, per_seq=seq // tm, n_chunks=nck, final_norm=final_norm),
        grid=(t // tm,),
        in_specs=[pl.BlockSpec((tm, d), lambda i: (i, 0)), row, const(wg), const(wv), const(cw), const(cb),
                  const(wd), row],
        out_specs=pl.BlockSpec((tm, d), lambda i: (i, 0)),
        out_shape=jax.ShapeDtypeStruct((t, d), F32),
        scratch_shapes=[pltpu.VMEM((nck, SUBLANES, fc), F32), pltpu.VMEM((tm, d), F32)],
        compiler_params=_cparams("arbitrary"), name="conv_glu",
    )(h2d, g_ffn.reshape(1, d), wg, wv, cw, cb, wd, g_final.reshape(1, d))


def _rope_tables(seq, head_dim, width):
    half = head_dim // 2
    inv = ROPE_THETA ** (-jnp.arange(half, dtype=F32) / half)
    ang = jnp.arange(seq, dtype=F32)[:, None] * inv[None, :]
    reps = width // half
    return jnp.tile(jnp.cos(ang), (1, reps)), jnp.tile(jnp.sin(ang), (1, reps))


def _split_w_in(w):
    ret = w[:, 0:768]
    rw = w[:, 768:1792]
    ssm = jnp.pad(w[:, 1792:2820], ((0, 0), (0, 1152 - 1028)))
    d = w[:, 2820:3368]
    dsa = jnp.concatenate([d[:, 0:256], d[:, 384:512], d[:, 256:384], d[:, 512:548],
                           jnp.zeros((w.shape[0], 128 - 36), w.dtype)], axis=1)
    return [t.astype(BF16) for t in (ret, rw, ssm, dsa)]


def kernel(x, mem, norm_mix, w_in, rwkv_mu, rwkv_w0, rwkv_w2, rwkv_a0, rwkv_a2, rwkv_g2, rwkv_k_k, rwkv_k_a, rwkv_r_k, rwkv_ln_w, rwkv_ln_b, ssm_conv_w, ssm_conv_b, ssm_dt_bias, ssm_a_log, ssm_d, ssm_norm, idx_k_norm, w_out, norm_cross, norm_mem, wq_x, wk_x, wv_x, wo_x, norm_ffn, w_up, ffn_conv_w, ffn_conv_b, w_down, norm_final):
    b, s, d = x.shape
    n_mem = mem.shape[1]
    depth = w_in.shape[0]
    t = b * s
    tm = min(512, s)
    cos32, sin32 = _rope_tables(s, RET_QK_DIM, 128)
    cos64, sin64 = _rope_tables(s, HEAD_DIM, 128)
    mem2d = mem.reshape(b * n_mem, d)
    h = x.reshape(t, d)
    for l in range(depth):
        ret_p, rw_p, ssm_p, dsa_p = _norm_matmul(h, norm_mix[l], _split_w_in(w_in[l]), [F32] * 4, tm)
        o_ret = _retention(ret_p.reshape(b, s, -1), cos32, sin32)
        o_rw = _rwkv(rw_p.reshape(b, s, -1), rwkv_mu[l], rwkv_w0[l], rwkv_w2[l], rwkv_a0[l], rwkv_a2[l],
                     rwkv_g2[l], rwkv_k_k[l], rwkv_k_a[l], rwkv_r_k[l], rwkv_ln_w[l], rwkv_ln_b[l])
        o_ssm = _ssd(ssm_p.reshape(b, s, -1), ssm_conv_w[l], ssm_conv_b[l], ssm_dt_bias[l], ssm_a_log[l],
                     ssm_d[l], ssm_norm[l])
        o_dsa = _dsa(dsa_p.reshape(b, s, -1), idx_k_norm[l], cos32, sin32, cos64, sin64)
        k_mem, v_mem = _norm_matmul(mem2d, norm_mem[l], [wk_x[l].astype(BF16), wv_x[l].astype(BF16)],
                                    [BF16, BF16], min(512, b * n_mem))
        outs = [o.reshape(t, GROUP_WIDTH) for o in (o_ret, o_rw, o_ssm, o_dsa)]
        h = _mix_cross(h, outs, w_out[l].astype(BF16), norm_cross[l], wq_x[l].astype(BF16),
                       k_mem.reshape(b, n_mem, d), v_mem.reshape(b, n_mem, d), wo_x[l].astype(BF16), s, tm)
        h = _ffn(h, norm_ffn[l], w_up[l], ffn_conv_w[l], ffn_conv_b[l], w_down[l], norm_final, s, tm,
                 final_norm=(l == depth - 1))
    return h.reshape(b, s, d)
```

```python
import functools
import math

import jax
import jax.numpy as jnp
import numpy as np
from jax import lax
from jax.experimental import pallas as pl
from jax.experimental.pallas import tpu as pltpu

F32 = jnp.float32
BF16 = jnp.bfloat16
I32 = jnp.int32
I16 = jnp.int16

D_MODEL = 1024
GROUP_WIDTH = 256
HEAD_DIM = 64
N_HEADS = 4
RET_QK_DIM = 32
RW_DECAY_SCALE = math.exp(-0.5)
RW_GN_EPS = 64e-5
SSM_STATE = 128
SSM_CONV = 4
SSM_XBC = 768
IDX_HEADS = 4
IDX_DIM = 32
DSA_TOPK_MAX = 256
DSA_QBLOCK = 256
X_HEADS = 4
X_HEAD_DIM = 256
D_FF = 2816
FFN_CONV = 3
ROPE_THETA = 10000.0
NORM_EPS = 1e-6

VMEM_LIMIT_BYTES = 52 * 1024 * 1024
SUBLANES = 8
LANES = 128

RET_CHUNK = 256
SSD_CHUNK = 128
RW_CHUNK = 64
RW_SEQS_PER_STEP = 8
DSA_KEY_CHUNK = 512
FF_CHUNK = 256


def _cparams(*sem):
    return pltpu.CompilerParams(dimension_semantics=sem, vmem_limit_bytes=VMEM_LIMIT_BYTES)


def _dot(a, b):
    return jnp.dot(a, b, preferred_element_type=F32)


def _dot_nt(a, b):
    return lax.dot_general(a, b, (((1,), (1,)), ((), ())), preferred_element_type=F32)


def _dot_tn(a, b):
    return lax.dot_general(a, b, (((0,), (0,)), ((), ())), preferred_element_type=F32)


def _split(a):
    hi = a.astype(BF16)
    lo = (a - hi.astype(F32)).astype(BF16)
    return hi, lo


def _dot_exact_rhs(a, b_bf16):
    hi, lo = _split(a)
    return _dot(hi, b_bf16) + _dot(lo, b_bf16)


def _dot3(a, b):
    ah, al = _split(a)
    bh, bl = _split(b)
    return _dot(ah, bh) + (_dot(ah, bl) + _dot(al, bh))


def _rms(x, g):
    return x * lax.rsqrt(jnp.mean(x * x, axis=-1, keepdims=True) + NORM_EPS) * g


def _silu(x):
    return x * jax.nn.sigmoid(x)


def _block_diag_ones(n, group_shift):
    r = lax.broadcasted_iota(I32, (n, n), 0) >> group_shift
    c = lax.broadcasted_iota(I32, (n, n), 1) >> group_shift
    return jnp.where(r == c, 1.0, 0.0).astype(BF16)


def _rot_half(x, half):
    n = x.shape[-1]
    lane = lax.broadcasted_iota(I32, x.shape, 1)
    first = (lane & (2 * half - 1)) < half
    return jnp.where(first, -pltpu.roll(x, n - half, 1), pltpu.roll(x, half, 1))


def _shift_rows(tail8, x, m):
    c = x.shape[0]
    xx = jnp.concatenate([tail8, x], axis=0)
    return pltpu.roll(xx, m, 0)[SUBLANES:SUBLANES + c]


def _cumsum_rows(x):
    c = x.shape[0]
    row = lax.broadcasted_iota(I32, x.shape, 0)
    d = 1
    while d < c:
        x = x + jnp.where(row >= d, pltpu.roll(x, d, 0), 0.0)
        d *= 2
    return x


def _head_select(cols, shift, width):
    c = cols[0].shape[0]
    lane_head = lax.broadcasted_iota(I32, (c, width), 1) >> shift
    out = jnp.zeros((c, width), F32)
    for h, col in enumerate(cols):
        out = jnp.where(lane_head == h, col, out)
    return out


def _norm_matmul_kernel(x_ref, g_ref, *refs, n_w):
    w_refs, o_refs = refs[:n_w], refs[n_w:]
    xb = _rms(x_ref[...], g_ref[...]).astype(BF16)
    for w_ref, o_ref in zip(w_refs, o_refs):
        o_ref[...] = _dot(xb, w_ref[...]).astype(o_ref.dtype)


def _norm_matmul(x2d, gain, weights, out_dtypes, tm):
    t, d = x2d.shape
    n_w = len(weights)
    in_specs = [pl.BlockSpec((tm, d), lambda i: (i, 0)), pl.BlockSpec((1, d), lambda i: (0, 0))]
    in_specs += [pl.BlockSpec(w.shape, lambda i: (0, 0)) for w in weights]
    out_specs = [pl.BlockSpec((tm, w.shape[1]), lambda i: (i, 0)) for w in weights]
    out_shape = [jax.ShapeDtypeStruct((t, w.shape[1]), dt) for w, dt in zip(weights, out_dtypes)]
    return pl.pallas_call(
        functools.partial(_norm_matmul_kernel, n_w=n_w),
        grid=(t // tm,), in_specs=in_specs, out_specs=out_specs, out_shape=out_shape,
        compiler_params=_cparams("parallel"), name="norm_matmul",
    )(x2d, gain.reshape(1, d), *weights)


def _ret_kernel(blk_ref, cos_ref, sin_ref, o_ref, state_ref, *, chunk):
    c = chunk

    @pl.when(pl.program_id(1) == 0)
    def _():
        state_ref[...] = jnp.zeros_like(state_ref)

    cos, sin = cos_ref[...], sin_ref[...]
    q = blk_ref[:, 0:128]
    k = blk_ref[:, 128:256]
    v = blk_ref[:, 256:512]
    g = blk_ref[:, 512:768]
    q = q * cos + _rot_half(q, RET_QK_DIM // 2) * sin
    k = (k * cos + _rot_half(k, RET_QK_DIM // 2) * sin) * (RET_QK_DIM ** -0.5)

    log_gamma = [math.log(1.0 - 2.0 ** (-5.0 - h)) for h in range(N_HEADS)]
    lane_k = lax.broadcasted_iota(I32, (1, 128), 1) >> 5
    lg_lane = jnp.zeros((1, 128), F32)
    for h in range(N_HEADS):
        lg_lane = jnp.where(lane_k == h, log_gamma[h], lg_lane)
    t_col = lax.broadcasted_iota(I32, (c, 1), 0).astype(F32)
    qd = q * jnp.exp((t_col + 1.0) * lg_lane)
    kd = k * jnp.exp((float(c - 1) - t_col) * lg_lane)

    ti = lax.broadcasted_iota(I32, (c, c), 0)
    si = lax.broadcasted_iota(I32, (c, c), 1)
    causal = ti >= si
    diff = jnp.where(causal, ti - si, 0).astype(F32)
    lane_qk = lax.broadcasted_iota(I32, (c, 128), 1) >> 5
    lane_v = lax.broadcasted_iota(I32, (c, 256), 1) >> 6
    kb = k.astype(BF16)
    vb = v.astype(BF16)
    y = _dot(qd.astype(BF16), state_ref[...].astype(BF16))
    for h in range(N_HEADS):
        qm = jnp.where(lane_qk == h, q, 0.0).astype(BF16)
        s = _dot_nt(qm, kb)
        dec = jnp.where(causal, jnp.exp(diff * log_gamma[h]), 0.0)
        yh = _dot((s * dec).astype(BF16), vb)
        y = y + jnp.where(lane_v == h, yh, 0.0)

    row_h = lax.broadcasted_iota(I32, (128, 256), 0) >> 5
    col_h = lax.broadcasted_iota(I32, (128, 256), 1) >> 6
    lg_row = jnp.zeros((128, 256), F32)
    for h in range(N_HEADS):
        lg_row = jnp.where(row_h == h, log_gamma[h], lg_row)
    new = _dot_tn(kd.astype(BF16), vb)
    state_ref[...] = state_ref[...] * jnp.exp(float(c) * lg_row) + jnp.where(row_h == col_h, new, 0.0)

    ms = _dot_exact_rhs(y * y, _block_diag_ones(256, 6)) * (1.0 / HEAD_DIM)
    o_ref[...] = y * lax.rsqrt(ms + NORM_EPS) * _silu(g)


def _retention(ret_p, cos32, sin32):
    b, s, _ = ret_p.shape
    c = min(RET_CHUNK, s)
    return pl.pallas_call(
        functools.partial(_ret_kernel, chunk=c),
        grid=(b, s // c),
        in_specs=[pl.BlockSpec((None, c, 768), lambda i, j: (i, j, 0)),
                  pl.BlockSpec((c, 128), lambda i, j: (j, 0)),
                  pl.BlockSpec((c, 128), lambda i, j: (j, 0))],
        out_specs=pl.BlockSpec((None, c, 256), lambda i, j: (i, j, 0)),
        out_shape=jax.ShapeDtypeStruct((b, s, 256), F32),
        scratch_shapes=[pltpu.VMEM((128, 256), F32)],
        compiler_params=_cparams("parallel", "arbitrary"), name="retention",
    )(ret_p, cos32, sin32)


def _ssd_kernel(blk_ref, cw_ref, cb_ref, dtb_ref, aneg_ref, dsk_ref, nw_ref, o_ref,
                state_ref, tail_ref, *, chunk):
    c = chunk

    @pl.when(pl.program_id(1) == 0)
    def _():
        state_ref[...] = jnp.zeros_like(state_ref)
        tail_ref[...] = jnp.zeros_like(tail_ref)

    z = blk_ref[:, 0:256]
    xbc = blk_ref[:, 256:1024]
    dt_raw = blk_ref[:, 1024:1152]

    tail = tail_ref[...]
    conv = xbc * cw_ref[SSM_CONV - 1:SSM_CONV, :] + cb_ref[...]
    for m in range(1, SSM_CONV):
        conv = conv + _shift_rows(tail, xbc, m) * cw_ref[SSM_CONV - 1 - m:SSM_CONV - m, :]
    tail_ref[...] = xbc[c - SUBLANES:c, :]
    xbc = _silu(conv)
    xs = xbc[:, 0:256]
    bm = xbc[:, 256:512]
    cm = xbc[:, 512:768]

    u = dt_raw + dtb_ref[...]
    dt = jnp.maximum(u, 0.0) + jnp.log1p(jnp.exp(-jnp.abs(u)))
    la = dt * aneg_ref[...]
    cum = _cumsum_rows(la)
    cum_t = cum.T
    cum_last = cum[c - 1:c, :]

    dt_lane = _head_select([dt[:, h:h + 1] for h in range(N_HEADS)], 6, 256)
    cum_lane = _head_select([cum[:, h:h + 1] for h in range(N_HEADS)], 6, 256)
    end_lane = _head_select([jnp.broadcast_to(cum_last[:, h:h + 1], (c, 1)) for h in range(N_HEADS)], 6, 256)
    xdt = xs * dt_lane
    x_end = (xdt * jnp.exp(end_lane - cum_lane)).astype(BF16)
    xdt_b = xdt.astype(BF16)
    e_lane = jnp.exp(cum_lane)
    chunk_decay = jnp.exp(end_lane[0:1, :])

    ti = lax.broadcasted_iota(I32, (c, c), 0)
    si = lax.broadcasted_iota(I32, (c, c), 1)
    causal = ti >= si
    lane_half = lax.broadcasted_iota(I32, (c, 128), 1) >> 6
    ys = []
    for grp in range(2):
        sl = slice(128 * grp, 128 * grp + 128)
        cg = cm[:, sl].astype(BF16)
        bg = bm[:, sl].astype(BF16)
        s = _dot_nt(cg, bg)
        parts = []
        for hh in range(2):
            h = 2 * grp + hh
            seg = jnp.minimum(cum[:, h:h + 1] - cum_t[h:h + 1, :], 0.0)
            dec = jnp.where(causal, jnp.exp(seg), 0.0)
            parts.append(_dot((s * dec).astype(BF16), xdt_b[:, sl]))
        y_g = jnp.where(lane_half == 0, parts[0], parts[1])
        st = state_ref[grp]
        y_g = y_g + e_lane[:, sl] * _dot(cg, st.astype(BF16))
        state_ref[grp] = st * chunk_decay[:, sl] + _dot_tn(bg, x_end[:, sl])
        ys.append(y_g)
    y = jnp.concatenate(ys, axis=1)
    y = (y + xs * dsk_ref[...]) * _silu(z)
    o_ref[...] = _rms(y, nw_ref[...])


def _ssd(ssm_p, conv_w, conv_b, dt_bias, a_log, d_skip, norm_w):
    b, s, _ = ssm_p.shape
    c = min(SSD_CHUNK, s)
    pad4 = lambda v: jnp.pad(v.astype(F32), (0, 128 - N_HEADS)).reshape(1, 128)
    params = [conv_w.astype(F32), conv_b.reshape(1, SSM_XBC), pad4(dt_bias),
              pad4(-jnp.exp(a_log.astype(F32))),
              jnp.repeat(d_skip, HEAD_DIM).reshape(1, 256), norm_w.reshape(1, 256)]
    full = lambda a: pl.BlockSpec(a.shape, lambda i, j: (0,) * a.ndim)
    return pl.pallas_call(
        functools.partial(_ssd_kernel, chunk=c),
        grid=(b, s // c),
        in_specs=[pl.BlockSpec((None, c, 1152), lambda i, j: (i, j, 0))] + [full(p) for p in params],
        out_specs=pl.BlockSpec((None, c, 256), lambda i, j: (i, j, 0)),
        out_shape=jax.ShapeDtypeStruct((b, s, 256), F32),
        scratch_shapes=[pltpu.VMEM((2, 128, 128), F32), pltpu.VMEM((SUBLANES, SSM_XBC), F32)],
        compiler_params=_cparams("parallel", "arbitrary"), name="ssd",
    )(ssm_p, *params)


def _rwkv_chunks(fs, tails, m0s, prm, c):
    mu, w0, w2, a0, a2, g2, k_k, k_a, r_k, ln_w, ln_b = prm
    n = N_HEADS * c
    mp = lambda f, *ls: [f(*a) for a in zip(*ls)]
    bd = _block_diag_ones(256, 6)
    lane_h = lax.broadcasted_iota(I32, (c, 256), 1) >> 6
    row = lax.broadcasted_iota(I32, (n, n), 0)
    col = lax.broadcasted_iota(I32, (n, n), 1)
    strict = row > col
    incl = row >= col
    inv_n = 1.0 / HEAD_DIM

    def stack(t):
        return jnp.concatenate([jnp.where(lane_h == h, t, 0.0) for h in range(N_HEADS)], axis=0).astype(BF16)

    xs = mp(lambda f, tail: f + (_shift_rows(tail, f, 1) - f) * mu, fs, tails)
    rs = [x[:, 0:256] for x in xs]
    ks = [x[:, 256:512] for x in xs]
    vs = [x[:, 512:768] for x in xs]
    loras = [x[:, 768:896] for x in xs]
    gls = [x[:, 896:1024] for x in xs]

    logws = mp(lambda lo: -RW_DECAY_SCALE * jax.nn.sigmoid(w0 + _dot3(jnp.tanh(lo), w2)), loras)
    a_s_ = mp(lambda lo: jax.nn.sigmoid(a0 + _dot3(lo, a2)), loras)
    gates = mp(lambda gl: _dot(jax.nn.sigmoid(gl).astype(BF16), g2), gls)
    kks = mp(lambda k: k * k_k, ks)
    kks = mp(lambda kk: kk * lax.rsqrt(_dot_exact_rhs(kk * kk, bd) + 1e-12), kks)
    k2s = mp(lambda k, a: k * (1.0 + (a - 1.0) * k_a), ks, a_s_)

    cums = mp(_cumsum_rows, logws)
    g_incs = mp(jnp.exp, cums)
    g_excs = mp(lambda cum, lw: jnp.exp(cum - lw), cums, logws)
    g_invs = mp(lambda cum: jnp.exp(-cum), cums)
    g_ends = mp(lambda cum: jnp.exp(cum[c - 1:c, :] - cum), cums)

    As = mp(lambda kk, g: stack(-kk * g), kks, g_excs)
    Bs = mp(lambda kk, a, g: stack(kk * a * g), kks, a_s_, g_invs)
    Ks = mp(lambda k2, g: stack(k2 * g), k2s, g_invs)
    Rs = mp(lambda r, g: stack(r * g), rs, g_incs)
    Vs = mp(stack, vs)
    Bends = mp(lambda kk, a, g: stack(kk * a * g), kks, a_s_, g_ends)
    Kends = mp(lambda k2, g: stack(k2 * g), k2s, g_ends)

    ps = mp(lambda A, B: jnp.where(strict, _dot_nt(A, B), 0.0).astype(BF16), As, Bs)
    l_aks = mp(lambda A, K: jnp.where(strict, _dot_nt(A, K), 0.0).astype(BF16), As, Ks)
    p_rbs = mp(lambda R, B: jnp.where(incl, _dot_nt(R, B), 0.0).astype(BF16), Rs, Bs)
    p_rks = mp(lambda R, K: jnp.where(incl, _dot_nt(R, K), 0.0).astype(BF16), Rs, Ks)

    m0bs = mp(lambda m0: m0.astype(BF16), m0s)
    us = mp(lambda A, m0b, l_ak, V: _dot_nt(A, m0b) + _dot(l_ak, V), As, m0bs, l_aks, Vs)
    steps = max(1, int(math.ceil(math.log2(c))))
    for i in range(steps):
        us = mp(lambda u, p: u + _dot(p, u.astype(BF16)), us, ps)
        if i + 1 < steps:
            ps = mp(lambda p: _dot(p, p).astype(BF16), ps)
    ubs = mp(lambda u: u.astype(BF16), us)
    y_ss = mp(lambda R, m0b, p_rb, ub, p_rk, V: _dot_nt(R, m0b) + _dot(p_rb, ub) + _dot(p_rk, V),
              Rs, m0bs, p_rbs, ubs, p_rks, Vs)

    def unstack(y_s):
        y = y_s[0:c]
        for h in range(1, N_HEADS):
            y = y + y_s[h * c:(h + 1) * c]
        return y

    ys = mp(unstack, y_ss)

    m_news = mp(lambda m0, g, Be, ub, Ke, V: m0 * g[c - 1:c, :] + _dot_tn(ub, Be) + _dot_tn(V, Ke),
                m0s, g_incs, Bends, ubs, Kends, Vs)

    means = mp(lambda y: _dot_exact_rhs(y, bd) * inv_n, ys)
    ycs = mp(lambda y, m: y - m, ys, means)
    vars_ = mp(lambda yc: _dot_exact_rhs(yc * yc, bd) * inv_n, ycs)
    yns = mp(lambda yc, var: yc * lax.rsqrt(var + RW_GN_EPS) * ln_w + ln_b, ycs, vars_)
    bonuses = mp(lambda r, k2, v: _dot_exact_rhs(r * k2 * r_k, bd) * v, rs, k2s, vs)
    outs = mp(lambda yn, bonus, gate: (yn + bonus) * gate, yns, bonuses, gates)
    return outs, m_news


def _rwkv_kernel(blk_ref, mu_ref, w0_ref, w2_ref, a0_ref, a2_ref, g2_ref, kk_ref, ka_ref, rk_ref,
                 lnw_ref, lnb_ref, o_ref, state_ref, prev_ref, *, chunk, n_seq):
    c = chunk

    @pl.when(pl.program_id(1) == 0)
    def _():
        state_ref[...] = jnp.zeros_like(state_ref)
        prev_ref[...] = jnp.zeros_like(prev_ref)

    prm = (mu_ref[...], w0_ref[...], w2_ref[...], a0_ref[...], a2_ref[...], g2_ref[...].astype(BF16),
           kk_ref[...], ka_ref[...], rk_ref[...], lnw_ref[...], lnb_ref[...])
    fs = [blk_ref[i] for i in range(n_seq)]
    outs, m_news = _rwkv_chunks(fs, [prev_ref[i] for i in range(n_seq)],
                                [state_ref[i] for i in range(n_seq)], prm, c)
    for i in range(n_seq):
        prev_ref[i] = fs[i][c - SUBLANES:c, :]
        state_ref[i] = m_news[i]
        o_ref[i] = outs[i]


def _rwkv(rw_p, mu, w0, w2, a0, a2, g2, k_k, k_a, r_k, ln_w, ln_b):
    b, s, _ = rw_p.shape
    c = min(RW_CHUNK, s)
    row = lambda v: v.astype(F32).reshape(1, -1)
    w2p = jnp.concatenate([w2, jnp.zeros_like(w2)], axis=0)
    a2p = jnp.concatenate([jnp.zeros_like(a2), a2], axis=0)
    params = [row(mu), row(w0), w2p, row(a0), a2p, g2, row(k_k), row(k_a), row(r_k), row(ln_w), row(ln_b)]
    full = lambda a: pl.BlockSpec(a.shape, lambda i, j: (0,) * a.ndim)
    n_seq = RW_SEQS_PER_STEP if b % RW_SEQS_PER_STEP == 0 else 1
    return pl.pallas_call(
        functools.partial(_rwkv_kernel, chunk=c, n_seq=n_seq),
        grid=(b // n_seq, s // c),
        in_specs=[pl.BlockSpec((n_seq, c, 1024), lambda i, j: (i, j, 0))] + [full(p) for p in params],
        out_specs=pl.BlockSpec((n_seq, c, 256), lambda i, j: (i, j, 0)),
        out_shape=jax.ShapeDtypeStruct((b, s, 256), F32),
        scratch_shapes=[pltpu.VMEM((n_seq, 256, 256), F32), pltpu.VMEM((n_seq, SUBLANES, 1024), F32)],
        compiler_params=_cparams("parallel", "arbitrary"), name="rwkv7",
    )(rw_p, *params)


def _dsa_kernel(q_ref, iq_ref, kv_ref, iki_ref, cos32_ref, sin32_ref, cos64_ref, sin64_ref, g_ref,
                o_ref, ik3_ref, kx_ref, vx_ref, keys_ref, keyt_ref, hi_ref, lo_ref, m_ref, acc_ref, s_ref, p_ref,
                *, seq, topk, kc):
    qb = DSA_QBLOCK
    j = pl.program_id(1)
    n_kc = (j * qb + qb + kc - 1) // kc

    @pl.when(j == 0)
    def _():
        lane = lax.broadcasted_iota(I32, (seq, 128), 1)
        kv = kv_ref[...]
        kr = kv * cos64_ref[...] + _rot_half(kv, HEAD_DIM // 2) * sin64_ref[...]
        kx_ref[...] = jnp.where(lane < HEAD_DIM, kr, 0.0).T.astype(BF16)
        vx_ref[...] = jnp.where(lane < HEAD_DIM, pltpu.roll(kv, 64, 1), 1.0).astype(BF16)
        ik = jnp.where(lane < IDX_DIM, iki_ref[...], 0.0)
        ik = ik * lax.rsqrt(jnp.sum(ik * ik, axis=-1, keepdims=True) * (1.0 / IDX_DIM) + NORM_EPS) * g_ref[...]
        ik = ik * cos32_ref[...] + _rot_half(ik, IDX_DIM // 2) * sin32_ref[...]
        ik = jnp.where(lane < IDX_DIM, ik, 0.0)
        hi, lo = _split(ik)
        hi, lo = hi.astype(F32), lo.astype(F32)
        ik3_ref[...] = (hi + pltpu.roll(hi, 32, 1) + pltpu.roll(lo, 64, 1)).T.astype(BF16)

    q0 = pl.multiple_of(j * qb, qb)
    q_pos = q0 + lax.broadcasted_iota(I32, (qb, 1), 0)
    lane = lax.broadcasted_iota(I32, (qb, 128), 1)

    iq = iq_ref[...]
    iq = iq * cos32_ref[pl.ds(q0, qb), :] + _rot_half(iq, IDX_DIM // 2) * sin32_ref[pl.ds(q0, qb), :]
    iq_hi, iq_lo = _split(iq)
    iq_hi, iq_lo = iq_hi.astype(F32), iq_lo.astype(F32)
    iq_lhs = []
    for h in range(IDX_HEADS):
        mine = (lane >> 5) == h
        a0 = jnp.where(mine, iq_hi, 0.0)
        b0 = jnp.where(mine, iq_lo, 0.0)
        if h:
            a0 = pltpu.roll(a0, 128 - 32 * h, 1)
            b0 = pltpu.roll(b0, 128 - 32 * h, 1)
        iq_lhs.append((a0 + pltpu.roll(b0, 32, 1) + pltpu.roll(a0, 64, 1)).astype(BF16))
    iq_stack = jnp.concatenate(iq_lhs, axis=0)
    iw = iki_ref[pl.ds(q0, qb), :] * (IDX_HEADS ** -0.5 * IDX_DIM ** -0.5)
    iw_cols = [iw[:, IDX_DIM + h:IDX_DIM + h + 1] for h in range(IDX_HEADS)]
    col_iota = lax.broadcasted_iota(I32, (qb, kc), 1)

    def score_body(ci, carry):
        off = pl.multiple_of(ci * kc, kc)
        rel = jnp.maximum(_dot(iq_stack, ik3_ref[:, pl.ds(off, kc)]), 0.0)
        sc = rel[0:qb] * iw_cols[0]
        for h in range(1, IDX_HEADS):
            sc = sc + rel[h * qb:(h + 1) * qb] * iw_cols[h]
        sc = sc + 0.0
        sc = jnp.where(off + col_iota <= q_pos, sc, -jnp.inf)
        bits = lax.bitcast_convert_type(sc, I32)
        key = bits ^ ((bits >> 31) & 0x7FFFFFFF)
        keys_ref[:, pl.ds(off, kc)] = key
        key_t = key.T
        keyt_ref[pl.ds(off, kc), :] = key_t
        hi_ref[pl.ds(off, kc), :] = (key_t >> 16).astype(I16)
        return carry

    lax.fori_loop(0, n_kc, score_body, 0)

    def count(ref, pred, one, zero, rows):
        def body(ci, acc):
            off = pl.multiple_of(ci * kc, kc)
            x = jnp.where(pred(ref[pl.ds(off, kc), :], off), one, zero)
            parts = [x[r * rows:(r + 1) * rows] for r in range(kc // rows)]
            while len(parts) > 1:
                parts = [parts[i] + parts[i + 1] for i in range(0, len(parts), 2)]
            return acc + parts[0]
        acc = lax.fori_loop(0, n_kc, body, jnp.zeros((rows, qb), one.dtype))
        return jnp.sum(acc.astype(F32), axis=0, keepdims=True)

    def count16(ref, pred):
        return count(ref, pred, jnp.int16(1), jnp.int16(0), 2 * SUBLANES)

    def count32(pred):
        return count(keyt_ref, pred, jnp.float32(1), jnp.float32(0), SUBLANES)

    def search16(ref, want, known_u=0, n_bits=16):
        def bit_body(i, t_u):
            cand_u = t_u | lax.shift_left(jnp.int32(1), jnp.int32(n_bits - 1) - i)
            cand = (cand_u - 32768).astype(I16)
            cnt = count16(ref, lambda kk_, off: kk_ >= cand)
            return jnp.where(cnt >= want, cand_u, t_u)
        return lax.fori_loop(0, n_bits, bit_body, jnp.full((1, qb), known_u, I32)) - 32768

    t_hi = search16(hi_ref, float(topk))
    t_hi16 = t_hi.astype(I16)
    want_lo = float(topk) - count16(hi_ref, lambda kk_, off: kk_ > t_hi16)

    def lo_body(ci, carry):
        off = pl.multiple_of(ci * kc, kc)
        low = ((keyt_ref[pl.ds(off, kc), :] & 0xFFFF) - 32768).astype(I16)
        lo_ref[pl.ds(off, kc), :] = jnp.where(hi_ref[pl.ds(off, kc), :] == t_hi16, low, jnp.int16(-32768))
        return carry

    lax.fori_loop(0, n_kc, lo_body, 0)
    t_lo = search16(lo_ref, want_lo)
    tau_row = lax.shift_left(t_hi, 16) | (t_lo + 32768)
    need = float(topk) - count32(lambda kk_, off: kk_ > tau_row)

    idx_bits = max(1, int(math.ceil(math.log2(seq))))
    row_iota = lax.broadcasted_iota(I32, (kc, qb), 0)

    def tie_body(ci, carry):
        off = pl.multiple_of(ci * kc, kc)
        rev = (seq - 1) - (off + row_iota)
        lo_ref[pl.ds(off, kc), :] = jnp.where(keyt_ref[pl.ds(off, kc), :] == tau_row, rev, -1).astype(I16)
        return carry

    lax.fori_loop(0, n_kc, tie_body, 0)
    c_star_row = (seq - 1) - search16(lo_ref, need, known_u=32768, n_bits=idx_bits)

    def to_column(row):
        as_f32 = lax.bitcast_convert_type(jnp.broadcast_to(row, (LANES, qb)), F32)
        return lax.bitcast_convert_type(as_f32.T, I32)[:, 0:1]

    tau = to_column(tau_row)
    c_star = to_column(c_star_row)

    q = q_ref[...]
    cos_q = jnp.concatenate([cos64_ref[pl.ds(q0, qb), :]] * 2, axis=1)
    sin_q = jnp.concatenate([sin64_ref[pl.ds(q0, qb), :]] * 2, axis=1)
    q = (q * cos_q + _rot_half(q, HEAD_DIM // 2) * sin_q) * (HEAD_DIM ** -0.5)
    q_lhs = []
    for h in range(N_HEADS):
        half = q[:, 128 * (h >> 1):128 * (h >> 1) + 128]
        half = jnp.where((lane >> 6) == (h & 1), half, 0.0)
        q_lhs.append((pltpu.roll(half, 64, 1) if h & 1 else half).astype(BF16))
    neg = -1e30
    q_stack = jnp.concatenate(q_lhs, axis=0)
    m_ref[...] = jnp.full_like(m_ref, neg)
    acc_ref[...] = jnp.zeros_like(acc_ref)

    def attn_body(ci, carry):
        off = pl.multiple_of(ci * kc, kc)
        kk_ = keys_ref[:, pl.ds(off, kc)]
        colp = off + col_iota
        sel = ((kk_ > tau) | ((kk_ == tau) & (colp <= c_star))) & (colp <= q_pos)
        bias = jnp.where(sel, 0.0, neg)
        kx = kx_ref[:, pl.ds(off, kc)]
        vx = vx_ref[pl.ds(off, kc), :]
        s_ref[...] = _dot(q_stack, kx) + jnp.concatenate([bias] * N_HEADS, axis=0)
        m_old = m_ref[...]
        m_new = jnp.maximum(m_old, jnp.max(s_ref[...], axis=-1, keepdims=True))
        for t in range(kc // LANES):
            sl = slice(t * LANES, (t + 1) * LANES)
            p_ref[:, sl] = jnp.exp(s_ref[:, sl] - m_new).astype(BF16)
        acc_ref[...] = jnp.exp(m_old - m_new) * acc_ref[...] + _dot(p_ref[...], vx)
        m_ref[...] = m_new
        return carry

    lax.fori_loop(0, n_kc, attn_body, 0)
    halves = []
    for pair in range(N_HEADS // 2):
        outs = []
        for h in (2 * pair, 2 * pair + 1):
            a = acc_ref[h * qb:(h + 1) * qb]
            outs.append(a / pltpu.roll(a, 64, 1))
        halves.append(jnp.where(lane < HEAD_DIM, outs[0], pltpu.roll(outs[1], 64, 1)))
    o_ref[...] = jnp.concatenate(halves, axis=1)


def _dsa(dsa_p, idx_k_norm, cos32, sin32, cos64, sin64):
    b, s, _ = dsa_p.shape
    qb = DSA_QBLOCK
    kc = min(DSA_KEY_CHUNK, s)
    topk = min(DSA_TOPK_MAX, s // 4)
    assert s % kc == 0 and s <= 2 ** 15, "key positions are searched as packed 16-bit values"
    gpad = jnp.pad(idx_k_norm.astype(F32), (0, 128 - IDX_DIM)).reshape(1, 128)
    full = lambda a: pl.BlockSpec(a.shape, lambda i, j: (0,) * a.ndim)
    return pl.pallas_call(
        functools.partial(_dsa_kernel, seq=s, topk=topk, kc=kc),
        grid=(b, s // qb),
        in_specs=[pl.BlockSpec((None, qb, 256), lambda i, j: (i, j, 0)),
                  pl.BlockSpec((None, qb, 128), lambda i, j: (i, j, 2)),
                  pl.BlockSpec((None, s, 128), lambda i, j: (i, 0, 3)),
                  pl.BlockSpec((None, s, 128), lambda i, j: (i, 0, 4)),
                  full(cos32), full(sin32), full(cos64), full(sin64), full(gpad)],
        out_specs=pl.BlockSpec((None, qb, 256), lambda i, j: (i, j, 0)),
        out_shape=jax.ShapeDtypeStruct((b, s, 256), F32),
        scratch_shapes=[pltpu.VMEM((128, s), BF16), pltpu.VMEM((128, s), BF16), pltpu.VMEM((s, 128), BF16),
                        pltpu.VMEM((qb, s), I32), pltpu.VMEM((s, qb), I32),
                        pltpu.VMEM((s, qb), I16), pltpu.VMEM((s, qb), I16),
                        pltpu.VMEM((N_HEADS * qb, LANES), F32), pltpu.VMEM((N_HEADS * qb, LANES), F32),
                        pltpu.VMEM((N_HEADS * qb, kc), F32), pltpu.VMEM((N_HEADS * qb, kc), BF16)],
        compiler_params=_cparams("parallel", "arbitrary"), name="dsa",
    )(dsa_p, dsa_p, dsa_p, dsa_p, cos32, sin32, cos64, sin64, gpad)


def _mix_cross_kernel(h_ref, o0_ref, o1_ref, o2_ref, o3_ref, wout_ref, g_ref, wq_ref, k_ref, v_ref, wo_ref,
                      out_ref):
    h = h_ref[...]
    for i, o_ref in enumerate((o0_ref, o1_ref, o2_ref, o3_ref)):
        h = h + _dot(o_ref[...].astype(BF16), wout_ref[i * GROUP_WIDTH:(i + 1) * GROUP_WIDTH, :])
    q = _dot(_rms(h, g_ref[...]).astype(BF16), wq_ref[...]) * (X_HEAD_DIM ** -0.5)
    outs = []
    for hd in range(X_HEADS):
        sl = slice(hd * X_HEAD_DIM, (hd + 1) * X_HEAD_DIM)
        s = _dot_nt(q[:, sl].astype(BF16), k_ref[:, sl])
        p = jnp.exp(s - jnp.max(s, axis=-1, keepdims=True))
        p = p / jnp.sum(p, axis=-1, keepdims=True)
        outs.append(_dot(p.astype(BF16), v_ref[:, sl]))
    o = jnp.concatenate(outs, axis=1).astype(BF16)
    out_ref[...] = h + _dot(o, wo_ref[...])


def _mix_cross(h2d, outs, w_out, g_cross, wq, k_mem, v_mem, wo, seq, tm):
    t, d = h2d.shape
    n_mem = k_mem.shape[1]
    per_seq = seq // tm
    tok = lambda w: pl.BlockSpec((tm, w), lambda i: (i, 0))
    const = lambda a: pl.BlockSpec(a.shape, lambda i: (0,) * a.ndim)
    mem = pl.BlockSpec((None, n_mem, d), lambda i: (i // per_seq, 0, 0))
    return pl.pallas_call(
        _mix_cross_kernel,
        grid=(t // tm,),
        in_specs=[tok(d)] + [tok(GROUP_WIDTH)] * 4 + [const(w_out), pl.BlockSpec((1, d), lambda i: (0, 0)),
                                                     const(wq), mem, mem, const(wo)],
        out_specs=tok(d),
        out_shape=jax.ShapeDtypeStruct((t, d), F32),
        compiler_params=_cparams("parallel"), name="mix_cross",
    )(h2d, *outs, w_out, g_cross.reshape(1, d), wq, k_mem, v_mem, wo)


def _ffn_kernel(h_ref, g_ref, wg_ref, wv_ref, cw_ref, cb_ref, wd_ref, gf_ref, out_ref, tail_ref, acc_ref,
                *, per_seq, n_chunks, final_norm):
    tm = h_ref.shape[0]

    @pl.when(pl.program_id(0) % per_seq == 0)
    def _():
        tail_ref[...] = jnp.zeros_like(tail_ref)

    h = h_ref[...]
    xb = _rms(h, g_ref[...]).astype(BF16)
    acc_ref[...] = jnp.zeros_like(acc_ref)

    def body(ci, carry):
        gate = _dot(xb, wg_ref[ci])
        val = _dot(xb, wv_ref[ci])
        tail = tail_ref[ci]
        cw = cw_ref[ci]
        conv = gate * cw[FFN_CONV - 1:FFN_CONV, :] + cb_ref[ci]
        for m in range(1, FFN_CONV):
            conv = conv + _shift_rows(tail, gate, m) * cw[FFN_CONV - 1 - m:FFN_CONV - m, :]
        tail_ref[ci] = gate[tm - SUBLANES:tm, :]
        acc_ref[...] += _dot((_silu(conv) * val).astype(BF16), wd_ref[ci])
        return carry

    lax.fori_loop(0, n_chunks, body, 0, unroll=True)
    out = h + acc_ref[...]
    if final_norm:
        out = _rms(out, gf_ref[...])
    out_ref[...] = out


def _ffn(h2d, g_ffn, w_up, conv_w, conv_b, w_down, g_final, seq, tm, final_norm):
    t, d = h2d.shape
    fc = FF_CHUNK
    nck = D_FF // fc
    chunked = lambda w: w.reshape(w.shape[0], nck, fc).transpose(1, 0, 2)
    wg = chunked(w_up[:, :D_FF]).astype(BF16)
    wv = chunked(w_up[:, D_FF:]).astype(BF16)
    cw = chunked(conv_w.astype(F32))
    cb = conv_b.astype(F32).reshape(nck, 1, fc)
    wd = w_down.reshape(nck, fc, d).astype(BF16)
    const = lambda a: pl.BlockSpec(a.shape, lambda i: (0,) * a.ndim)
    row = pl.BlockSpec((1, d), lambda i: (0, 0))
    return pl.pallas_call(
        functools.partial(_ffn_kernel, per_seq=seq // tm, n_chunks=nck, final_norm=final_norm),
        grid=(t // tm,),
        in_specs=[pl.BlockSpec((tm, d), lambda i: (i, 0)), row, const(wg), const(wv), const(cw), const(cb),
                  const(wd), row],
        out_specs=pl.BlockSpec((tm, d), lambda i: (i, 0)),
        out_shape=jax.ShapeDtypeStruct((t, d), F32),
        scratch_shapes=[pltpu.VMEM((nck, SUBLANES, fc), F32), pltpu.VMEM((tm, d), F32)],
        compiler_params=_cparams("arbitrary"), name="conv_glu",
    )(h2d, g_ffn.reshape(1, d), wg, wv, cw, cb, wd, g_final.reshape(1, d))


def _rope_tables(seq, head_dim, width):
    half = head_dim // 2
    inv = ROPE_THETA ** (-jnp.arange(half, dtype=F32) / half)
    ang = jnp.arange(seq, dtype=F32)[:, None] * inv[None, :]
    reps = width // half
    return jnp.tile(jnp.cos(ang), (1, reps)), jnp.tile(jnp.sin(ang), (1, reps))


def _split_w_in(w):
    ret = w[:, 0:768]
    rw = w[:, 768:1792]
    ssm = jnp.pad(w[:, 1792:2820], ((0, 0), (0, 1152 - 1028)))
    d = w[:, 2820:3368]
    dsa = jnp.concatenate([d[:, 0:256], d[:, 384:512], d[:, 256:384], d[:, 512:548],
                           jnp.zeros((w.shape[0], 128 - 36), w.dtype)], axis=1)
    return [t.astype(BF16) for t in (ret, rw, ssm, dsa)]


def kernel(x, mem, norm_mix, w_in, rwkv_mu, rwkv_w0, rwkv_w2, rwkv_a0, rwkv_a2, rwkv_g2, rwkv_k_k, rwkv_k_a, rwkv_r_k, rwkv_ln_w, rwkv_ln_b, ssm_conv_w, ssm_conv_b, ssm_dt_bias, ssm_a_log, ssm_d, ssm_norm, idx_k_norm, w_out, norm_cross, norm_mem, wq_x, wk_x, wv_x, wo_x, norm_ffn, w_up, ffn_conv_w, ffn_conv_b, w_down, norm_final):
    b, s, d = x.shape
    n_mem = mem.shape[1]
    depth = w_in.shape[0]
    t = b * s
    tm = min(512, s)
    cos32, sin32 = _rope_tables(s, RET_QK_DIM, 128)
    cos64, sin64 = _rope_tables(s, HEAD_DIM, 128)
    mem2d = mem.reshape(b * n_mem, d)
    h = x.reshape(t, d)
    for l in range(depth):
        ret_p, rw_p, ssm_p, dsa_p = _norm_matmul(h, norm_mix[l], _split_w_in(w_in[l]), [F32] * 4, tm)
        o_ret = _retention(ret_p.reshape(b, s, -1), cos32, sin32)
        o_rw = _rwkv(rw_p.reshape(b, s, -1), rwkv_mu[l], rwkv_w0[l], rwkv_w2[l], rwkv_a0[l], rwkv_a2[l],
                     rwkv_g2[l], rwkv_k_k[l], rwkv_k_a[l], rwkv_r_k[l], rwkv_ln_w[l], rwkv_ln_b[l])
        o_ssm = _ssd(ssm_p.reshape(b, s, -1), ssm_conv_w[l], ssm_conv_b[l], ssm_dt_bias[l], ssm_a_log[l],
                     ssm_d[l], ssm_norm[l])
        o_dsa = _dsa(dsa_p.reshape(b, s, -1), idx_k_norm[l], cos32, sin32, cos64, sin64)
        k_mem, v_mem = _norm_matmul(mem2d, norm_mem[l], [wk_x[l].astype(BF16), wv_x[l].astype(BF16)],
                                    [BF16, BF16], min(512, b * n_mem))
        outs = [o.reshape(t, GROUP_WIDTH) for o in (o_ret, o_rw, o_ssm, o_dsa)]
        h = _mix_cross(h, outs, w_out[l].astype(BF16), norm_cross[l], wq_x[l].astype(BF16),
                       k_mem.reshape(b, n_mem, d), v_mem.reshape(b, n_mem, d), wo_x[l].astype(BF16), s, tm)
        h = _ffn(h, norm_ffn[l], w_up[l], ffn_conv_w[l], ffn_conv_b[l], w_down[l], norm_final, s, tm,
                 final_norm=(l == depth - 1))
    return h.reshape(b, s, d)
```

```python
import functools
import math

import jax
import jax.numpy as jnp
import numpy as np
from jax import lax
from jax.experimental import pallas as pl
from jax.experimental.pallas import tpu as pltpu

F32 = jnp.float32
BF16 = jnp.bfloat16
I32 = jnp.int32
I16 = jnp.int16

D_MODEL = 1024
GROUP_WIDTH = 256
HEAD_DIM = 64
N_HEADS = 4
RET_QK_DIM = 32
RW_DECAY_SCALE = math.exp(-0.5)
RW_GN_EPS = 64e-5
SSM_STATE = 128
SSM_CONV = 4
SSM_XBC = 768
IDX_HEADS = 4
IDX_DIM = 32
DSA_TOPK_MAX = 256
DSA_QBLOCK = 256
X_HEADS = 4
X_HEAD_DIM = 256
D_FF = 2816
FFN_CONV = 3
ROPE_THETA = 10000.0
NORM_EPS = 1e-6

VMEM_LIMIT_BYTES = 52 * 1024 * 1024
SUBLANES = 8
LANES = 128

RET_CHUNK = 256
SSD_CHUNK = 128
RW_CHUNK = 64
RW_SEQS_PER_STEP = 8
DSA_KEY_CHUNK = 512
FF_CHUNK = 256


def _cparams(*sem):
    return pltpu.CompilerParams(dimension_semantics=sem, vmem_limit_bytes=VMEM_LIMIT_BYTES)


def _dot(a, b):
    return jnp.dot(a, b, preferred_element_type=F32)


def _dot_nt(a, b):
    return lax.dot_general(a, b, (((1,), (1,)), ((), ())), preferred_element_type=F32)


def _dot_tn(a, b):
    return lax.dot_general(a, b, (((0,), (0,)), ((), ())), preferred_element_type=F32)


def _split(a):
    hi = a.astype(BF16)
    lo = (a - hi.astype(F32)).astype(BF16)
    return hi, lo


def _dot_exact_rhs(a, b_bf16):
    hi, lo = _split(a)
    return _dot(hi, b_bf16) + _dot(lo, b_bf16)


def _dot3(a, b):
    ah, al = _split(a)
    bh, bl = _split(b)
    return _dot(ah, bh) + (_dot(ah, bl) + _dot(al, bh))


def _rms(x, g):
    return x * lax.rsqrt(jnp.mean(x * x, axis=-1, keepdims=True) + NORM_EPS) * g


def _silu(x):
    return x * jax.nn.sigmoid(x)


def _block_diag_ones(n, group_shift):
    r = lax.broadcasted_iota(I32, (n, n), 0) >> group_shift
    c = lax.broadcasted_iota(I32, (n, n), 1) >> group_shift
    return jnp.where(r == c, 1.0, 0.0).astype(BF16)


def _rot_half(x, half):
    n = x.shape[-1]
    lane = lax.broadcasted_iota(I32, x.shape, 1)
    first = (lane & (2 * half - 1)) < half
    return jnp.where(first, -pltpu.roll(x, n - half, 1), pltpu.roll(x, half, 1))


def _shift_rows(tail8, x, m):
    c = x.shape[0]
    xx = jnp.concatenate([tail8, x], axis=0)
    return pltpu.roll(xx, m, 0)[SUBLANES:SUBLANES + c]


def _cumsum_rows(x):
    c = x.shape[0]
    row = lax.broadcasted_iota(I32, x.shape, 0)
    d = 1
    while d < c:
        x = x + jnp.where(row >= d, pltpu.roll(x, d, 0), 0.0)
        d *= 2
    return x


def _head_select(cols, shift, width):
    c = cols[0].shape[0]
    lane_head = lax.broadcasted_iota(I32, (c, width), 1) >> shift
    out = jnp.zeros((c, width), F32)
    for h, col in enumerate(cols):
        out = jnp.where(lane_head == h, col, out)
    return out


def _norm_matmul_kernel(x_ref, g_ref, *refs, n_w):
    w_refs, o_refs = refs[:n_w], refs[n_w:]
    xb = _rms(x_ref[...], g_ref[...]).astype(BF16)
    for w_ref, o_ref in zip(w_refs, o_refs):
        o_ref[...] = _dot(xb, w_ref[...]).astype(o_ref.dtype)


def _norm_matmul(x2d, gain, weights, out_dtypes, tm):
    t, d = x2d.shape
    n_w = len(weights)
    in_specs = [pl.BlockSpec((tm, d), lambda i: (i, 0)), pl.BlockSpec((1, d), lambda i: (0, 0))]
    in_specs += [pl.BlockSpec(w.shape, lambda i: (0, 0)) for w in weights]
    out_specs = [pl.BlockSpec((tm, w.shape[1]), lambda i: (i, 0)) for w in weights]
    out_shape = [jax.ShapeDtypeStruct((t, w.shape[1]), dt) for w, dt in zip(weights, out_dtypes)]
    return pl.pallas_call(
        functools.partial(_norm_matmul_kernel, n_w=n_w),
        grid=(t // tm,), in_specs=in_specs, out_specs=out_specs, out_shape=out_shape,
        compiler_params=_cparams("parallel"), name="norm_matmul",
    )(x2d, gain.reshape(1, d), *weights)


def _ret_kernel(blk_ref, cos_ref, sin_ref, o_ref, state_ref, *, chunk):
    c = chunk

    @pl.when(pl.program_id(1) == 0)
    def _():
        state_ref[...] = jnp.zeros_like(state_ref)

    cos, sin = cos_ref[...], sin_ref[...]
    q = blk_ref[:, 0:128]
    k = blk_ref[:, 128:256]
    v = blk_ref[:, 256:512]
    g = blk_ref[:, 512:768]
    q = q * cos + _rot_half(q, RET_QK_DIM // 2) * sin
    k = (k * cos + _rot_half(k, RET_QK_DIM // 2) * sin) * (RET_QK_DIM ** -0.5)

    log_gamma = [math.log(1.0 - 2.0 ** (-5.0 - h)) for h in range(N_HEADS)]
    lane_k = lax.broadcasted_iota(I32, (1, 128), 1) >> 5
    lg_lane = jnp.zeros((1, 128), F32)
    for h in range(N_HEADS):
        lg_lane = jnp.where(lane_k == h, log_gamma[h], lg_lane)
    t_col = lax.broadcasted_iota(I32, (c, 1), 0).astype(F32)
    qd = q * jnp.exp((t_col + 1.0) * lg_lane)
    kd = k * jnp.exp((float(c - 1) - t_col) * lg_lane)

    ti = lax.broadcasted_iota(I32, (c, c), 0)
    si = lax.broadcasted_iota(I32, (c, c), 1)
    causal = ti >= si
    diff = jnp.where(causal, ti - si, 0).astype(F32)
    lane_qk = lax.broadcasted_iota(I32, (c, 128), 1) >> 5
    lane_v = lax.broadcasted_iota(I32, (c, 256), 1) >> 6
    kb = k.astype(BF16)
    vb = v.astype(BF16)
    y = _dot(qd.astype(BF16), state_ref[...].astype(BF16))
    for h in range(N_HEADS):
        qm = jnp.where(lane_qk == h, q, 0.0).astype(BF16)
        s = _dot_nt(qm, kb)
        dec = jnp.where(causal, jnp.exp(diff * log_gamma[h]), 0.0)
        yh = _dot((s * dec).astype(BF16), vb)
        y = y + jnp.where(lane_v == h, yh, 0.0)

    row_h = lax.broadcasted_iota(I32, (128, 256), 0) >> 5
    col_h = lax.broadcasted_iota(I32, (128, 256), 1) >> 6
    lg_row = jnp.zeros((128, 256), F32)
    for h in range(N_HEADS):
        lg_row = jnp.where(row_h == h, log_gamma[h], lg_row)
    new = _dot_tn(kd.astype(BF16), vb)
    state_ref[...] = state_ref[...] * jnp.exp(float(c) * lg_row) + jnp.where(row_h == col_h, new, 0.0)

    ms = _dot_exact_rhs(y * y, _block_diag_ones(256, 6)) * (1.0 / HEAD_DIM)
    o_ref[...] = y * lax.rsqrt(ms + NORM_EPS) * _silu(g)


def _retention(ret_p, cos32, sin32):
    b, s, _ = ret_p.shape
    c = min(RET_CHUNK, s)
    return pl.pallas_call(
        functools.partial(_ret_kernel, chunk=c),
        grid=(b, s // c),
        in_specs=[pl.BlockSpec((None, c, 768), lambda i, j: (i, j, 0)),
                  pl.BlockSpec((c, 128), lambda i, j: (j, 0)),
                  pl.BlockSpec((c, 128), lambda i, j: (j, 0))],
        out_specs=pl.BlockSpec((None, c, 256), lambda i, j: (i, j, 0)),
        out_shape=jax.ShapeDtypeStruct((b, s, 256), F32),
        scratch_shapes=[pltpu.VMEM((128, 256), F32)],
        compiler_params=_cparams("parallel", "arbitrary"), name="retention",
    )(ret_p, cos32, sin32)


def _ssd_kernel(blk_ref, cw_ref, cb_ref, dtb_ref, aneg_ref, dsk_ref, nw_ref, o_ref,
                state_ref, tail_ref, *, chunk):
    c = chunk

    @pl.when(pl.program_id(1) == 0)
    def _():
        state_ref[...] = jnp.zeros_like(state_ref)
        tail_ref[...] = jnp.zeros_like(tail_ref)

    z = blk_ref[:, 0:256]
    xbc = blk_ref[:, 256:1024]
    dt_raw = blk_ref[:, 1024:1152]

    tail = tail_ref[...]
    conv = xbc * cw_ref[SSM_CONV - 1:SSM_CONV, :] + cb_ref[...]
    for m in range(1, SSM_CONV):
        conv = conv + _shift_rows(tail, xbc, m) * cw_ref[SSM_CONV - 1 - m:SSM_CONV - m, :]
    tail_ref[...] = xbc[c - SUBLANES:c, :]
    xbc = _silu(conv)
    xs = xbc[:, 0:256]
    bm = xbc[:, 256:512]
    cm = xbc[:, 512:768]

    u = dt_raw + dtb_ref[...]
    dt = jnp.maximum(u, 0.0) + jnp.log1p(jnp.exp(-jnp.abs(u)))
    la = dt * aneg_ref[...]
    cum = _cumsum_rows(la)
    cum_t = cum.T
    cum_last = cum[c - 1:c, :]

    dt_lane = _head_select([dt[:, h:h + 1] for h in range(N_HEADS)], 6, 256)
    cum_lane = _head_select([cum[:, h:h + 1] for h in range(N_HEADS)], 6, 256)
    end_lane = _head_select([jnp.broadcast_to(cum_last[:, h:h + 1], (c, 1)) for h in range(N_HEADS)], 6, 256)
    xdt = xs * dt_lane
    x_end = (xdt * jnp.exp(end_lane - cum_lane)).astype(BF16)
    xdt_b = xdt.astype(BF16)
    e_lane = jnp.exp(cum_lane)
    chunk_decay = jnp.exp(end_lane[0:1, :])

    ti = lax.broadcasted_iota(I32, (c, c), 0)
    si = lax.broadcasted_iota(I32, (c, c), 1)
    causal = ti >= si
    lane_half = lax.broadcasted_iota(I32, (c, 128), 1) >> 6
    ys = []
    for grp in range(2):
        sl = slice(128 * grp, 128 * grp + 128)
        cg = cm[:, sl].astype(BF16)
        bg = bm[:, sl].astype(BF16)
        s = _dot_nt(cg, bg)
        parts = []
        for hh in range(2):
            h = 2 * grp + hh
            seg = jnp.minimum(cum[:, h:h + 1] - cum_t[h:h + 1, :], 0.0)
            dec = jnp.where(causal, jnp.exp(seg), 0.0)
            parts.append(_dot((s * dec).astype(BF16), xdt_b[:, sl]))
        y_g = jnp.where(lane_half == 0, parts[0], parts[1])
        st = state_ref[grp]
        y_g = y_g + e_lane[:, sl] * _dot(cg, st.astype(BF16))
        state_ref[grp] = st * chunk_decay[:, sl] + _dot_tn(bg, x_end[:, sl])
        ys.append(y_g)
    y = jnp.concatenate(ys, axis=1)
    y = (y + xs * dsk_ref[...]) * _silu(z)
    o_ref[...] = _rms(y, nw_ref[...])


def _ssd(ssm_p, conv_w, conv_b, dt_bias, a_log, d_skip, norm_w):
    b, s, _ = ssm_p.shape
    c = min(SSD_CHUNK, s)
    pad4 = lambda v: jnp.pad(v.astype(F32), (0, 128 - N_HEADS)).reshape(1, 128)
    params = [conv_w.astype(F32), conv_b.reshape(1, SSM_XBC), pad4(dt_bias),
              pad4(-jnp.exp(a_log.astype(F32))),
              jnp.repeat(d_skip, HEAD_DIM).reshape(1, 256), norm_w.reshape(1, 256)]
    full = lambda a: pl.BlockSpec(a.shape, lambda i, j: (0,) * a.ndim)
    return pl.pallas_call(
        functools.partial(_ssd_kernel, chunk=c),
        grid=(b, s // c),
        in_specs=[pl.BlockSpec((None, c, 1152), lambda i, j: (i, j, 0))] + [full(p) for p in params],
        out_specs=pl.BlockSpec((None, c, 256), lambda i, j: (i, j, 0)),
        out_shape=jax.ShapeDtypeStruct((b, s, 256), F32),
        scratch_shapes=[pltpu.VMEM((2, 128, 128), F32), pltpu.VMEM((SUBLANES, SSM_XBC), F32)],
        compiler_params=_cparams("parallel", "arbitrary"), name="ssd",
    )(ssm_p, *params)


def _rwkv_chunks(fs, tails, m0s, prm, c):
    mu, w0, w2, a0, a2, g2, k_k, k_a, r_k, ln_w, ln_b = prm
    n = N_HEADS * c
    mp = lambda f, *ls: [f(*a) for a in zip(*ls)]
    bd = _block_diag_ones(256, 6)
    lane_h = lax.broadcasted_iota(I32, (c, 256), 1) >> 6
    row = lax.broadcasted_iota(I32, (n, n), 0)
    col = lax.broadcasted_iota(I32, (n, n), 1)
    strict = row > col
    incl = row >= col
    inv_n = 1.0 / HEAD_DIM

    def stack(t):
        return jnp.concatenate([jnp.where(lane_h == h, t, 0.0) for h in range(N_HEADS)], axis=0).astype(BF16)

    xs = mp(lambda f, tail: f + (_shift_rows(tail, f, 1) - f) * mu, fs, tails)
    rs = [x[:, 0:256] for x in xs]
    ks = [x[:, 256:512] for x in xs]
    vs = [x[:, 512:768] for x in xs]
    loras = [x[:, 768:896] for x in xs]
    gls = [x[:, 896:1024] for x in xs]

    logws = mp(lambda lo: -RW_DECAY_SCALE * jax.nn.sigmoid(w0 + _dot3(jnp.tanh(lo), w2)), loras)
    a_s_ = mp(lambda lo: jax.nn.sigmoid(a0 + _dot3(lo, a2)), loras)
    gates = mp(lambda gl: _dot(jax.nn.sigmoid(gl).astype(BF16), g2), gls)
    kks = mp(lambda k: k * k_k, ks)
    kks = mp(lambda kk: kk * lax.rsqrt(_dot_exact_rhs(kk * kk, bd) + 1e-12), kks)
    k2s = mp(lambda k, a: k * (1.0 + (a - 1.0) * k_a), ks, a_s_)

    cums = mp(_cumsum_rows, logws)
    g_incs = mp(jnp.exp, cums)
    g_excs = mp(lambda cum, lw: jnp.exp(cum - lw), cums, logws)
    g_invs = mp(lambda cum: jnp.exp(-cum), cums)
    g_ends = mp(lambda cum: jnp.exp(cum[c - 1:c, :] - cum), cums)

    As = mp(lambda kk, g: stack(-kk * g), kks, g_excs)
    Bs = mp(lambda kk, a, g: stack(kk * a * g), kks, a_s_, g_invs)
    Ks = mp(lambda k2, g: stack(k2 * g), k2s, g_invs)
    Rs = mp(lambda r, g: stack(r * g), rs, g_incs)
    Vs = mp(stack, vs)
    Bends = mp(lambda kk, a, g: stack(kk * a * g), kks, a_s_, g_ends)
    Kends = mp(lambda k2, g: stack(k2 * g), k2s, g_ends)

    ps = mp(lambda A, B: jnp.where(strict, _dot_nt(A, B), 0.0).astype(BF16), As, Bs)
    l_aks = mp(lambda A, K: jnp.where(strict, _dot_nt(A, K), 0.0).astype(BF16), As, Ks)
    p_rbs = mp(lambda R, B: jnp.where(incl, _dot_nt(R, B), 0.0).astype(BF16), Rs, Bs)
    p_rks = mp(lambda R, K: jnp.where(incl, _dot_nt(R, K), 0.0).astype(BF16), Rs, Ks)

    m0bs = mp(lambda m0: m0.astype(BF16), m0s)
    us = mp(lambda A, m0b, l_ak, V: _dot_nt(A, m0b) + _dot(l_ak, V), As, m0bs, l_aks, Vs)
    steps = max(1, int(math.ceil(math.log2(c))))
    for i in range(steps):
        us = mp(lambda u, p: u + _dot(p, u.astype(BF16)), us, ps)
        if i + 1 < steps:
            ps = mp(lambda p: _dot(p, p).astype(BF16), ps)
    ubs = mp(lambda u: u.astype(BF16), us)
    y_ss = mp(lambda R, m0b, p_rb, ub, p_rk, V: _dot_nt(R, m0b) + _dot(p_rb, ub) + _dot(p_rk, V),
              Rs, m0bs, p_rbs, ubs, p_rks, Vs)

    def unstack(y_s):
        y = y_s[0:c]
        for h in range(1, N_HEADS):
            y = y + y_s[h * c:(h + 1) * c]
        return y

    ys = mp(unstack, y_ss)

    m_news = mp(lambda m0, g, Be, ub, Ke, V: m0 * g[c - 1:c, :] + _dot_tn(ub, Be) + _dot_tn(V, Ke),
                m0s, g_incs, Bends, ubs, Kends, Vs)

    means = mp(lambda y: _dot_exact_rhs(y, bd) * inv_n, ys)
    ycs = mp(lambda y, m: y - m, ys, means)
    vars_ = mp(lambda yc: _dot_exact_rhs(yc * yc, bd) * inv_n, ycs)
    yns = mp(lambda yc, var: yc * lax.rsqrt(var + RW_GN_EPS) * ln_w + ln_b, ycs, vars_)
    bonuses = mp(lambda r, k2, v: _dot_exact_rhs(r * k2 * r_k, bd) * v, rs, k2s, vs)
    outs = mp(lambda yn, bonus, gate: (yn + bonus) * gate, yns, bonuses, gates)
    return outs, m_news


def _rwkv_kernel(blk_ref, mu_ref, w0_ref, w2_ref, a0_ref, a2_ref, g2_ref, kk_ref, ka_ref, rk_ref,
                 lnw_ref, lnb_ref, o_ref, state_ref, prev_ref, *, chunk, n_seq):
    c = chunk

    @pl.when(pl.program_id(1) == 0)
    def _():
        state_ref[...] = jnp.zeros_like(state_ref)
        prev_ref[...] = jnp.zeros_like(prev_ref)

    prm = (mu_ref[...], w0_ref[...], w2_ref[...], a0_ref[...], a2_ref[...], g2_ref[...].astype(BF16),
           kk_ref[...], ka_ref[...], rk_ref[...], lnw_ref[...], lnb_ref[...])
    fs = [blk_ref[i] for i in range(n_seq)]
    outs, m_news = _rwkv_chunks(fs, [prev_ref[i] for i in range(n_seq)],
                                [state_ref[i] for i in range(n_seq)], prm, c)
    for i in range(n_seq):
        prev_ref[i] = fs[i][c - SUBLANES:c, :]
        state_ref[i] = m_news[i]
        o_ref[i] = outs[i]


def _rwkv(rw_p, mu, w0, w2, a0, a2, g2, k_k, k_a, r_k, ln_w, ln_b):
    b, s, _ = rw_p.shape
    c = min(RW_CHUNK, s)
    row = lambda v: v.astype(F32).reshape(1, -1)
    w2p = jnp.concatenate([w2, jnp.zeros_like(w2)], axis=0)
    a2p = jnp.concatenate([jnp.zeros_like(a2), a2], axis=0)
    params = [row(mu), row(w0), w2p, row(a0), a2p, g2, row(k_k), row(k_a), row(r_k), row(ln_w), row(ln_b)]
    full = lambda a: pl.BlockSpec(a.shape, lambda i, j: (0,) * a.ndim)
    n_seq = RW_SEQS_PER_STEP if b % RW_SEQS_PER_STEP == 0 else 1
    return pl.pallas_call(
        functools.partial(_rwkv_kernel, chunk=c, n_seq=n_seq),
        grid=(b // n_seq, s // c),
        in_specs=[pl.BlockSpec((n_seq, c, 1024), lambda i, j: (i, j, 0))] + [full(p) for p in params],
        out_specs=pl.BlockSpec((n_seq, c, 256), lambda i, j: (i, j, 0)),
        out_shape=jax.ShapeDtypeStruct((b, s, 256), F32),
        scratch_shapes=[pltpu.VMEM((n_seq, 256, 256), F32), pltpu.VMEM((n_seq, SUBLANES, 1024), F32)],
        compiler_params=_cparams("parallel", "arbitrary"), name="rwkv7",
    )(rw_p, *params)


def _dsa_kernel(q_ref, iq_ref, kv_ref, iki_ref, cos32_ref, sin32_ref, cos64_ref, sin64_ref, g_ref,
                o_ref, ik3_ref, kx_ref, vx_ref, keys_ref, keyt_ref, hi_ref, lo_ref, m_ref, acc_ref, s_ref, p_ref,
                *, seq, topk, kc):
    qb = DSA_QBLOCK
    j = pl.program_id(1)
    n_kc = (j * qb + qb + kc - 1) // kc

    @pl.when(j == 0)
    def _():
        lane = lax.broadcasted_iota(I32, (seq, 128), 1)
        kv = kv_ref[...]
        kr = kv * cos64_ref[...] + _rot_half(kv, HEAD_DIM // 2) * sin64_ref[...]
        kx_ref[...] = jnp.where(lane < HEAD_DIM, kr, 0.0).T.astype(BF16)
        vx_ref[...] = jnp.where(lane < HEAD_DIM, pltpu.roll(kv, 64, 1), 1.0).astype(BF16)
        ik = jnp.where(lane < IDX_DIM, iki_ref[...], 0.0)
        ik = ik * lax.rsqrt(jnp.sum(ik * ik, axis=-1, keepdims=True) * (1.0 / IDX_DIM) + NORM_EPS) * g_ref[...]
        ik = ik * cos32_ref[...] + _rot_half(ik, IDX_DIM // 2) * sin32_ref[...]
        ik = jnp.where(lane < IDX_DIM, ik, 0.0)
        hi, lo = _split(ik)
        hi, lo = hi.astype(F32), lo.astype(F32)
        ik3_ref[...] = (hi + pltpu.roll(hi, 32, 1) + pltpu.roll(lo, 64, 1)).T.astype(BF16)

    q0 = pl.multiple_of(j * qb, qb)
    q_pos = q0 + lax.broadcasted_iota(I32, (qb, 1), 0)
    lane = lax.broadcasted_iota(I32, (qb, 128), 1)

    iq = iq_ref[...]
    iq = iq * cos32_ref[pl.ds(q0, qb), :] + _rot_half(iq, IDX_DIM // 2) * sin32_ref[pl.ds(q0, qb), :]
    iq_hi, iq_lo = _split(iq)
    iq_hi, iq_lo = iq_hi.astype(F32), iq_lo.astype(F32)
    iq_lhs = []
    for h in range(IDX_HEADS):
        mine = (lane >> 5) == h
        a0 = jnp.where(mine, iq_hi, 0.0)
        b0 = jnp.where(mine, iq_lo, 0.0)
        if h:
            a0 = pltpu.roll(a0, 128 - 32 * h, 1)
            b0 = pltpu.roll(b0, 128 - 32 * h, 1)
        iq_lhs.append((a0 + pltpu.roll(b0, 32, 1) + pltpu.roll(a0, 64, 1)).astype(BF16))
    iq_stack = jnp.concatenate(iq_lhs, axis=0)
    iw = iki_ref[pl.ds(q0, qb), :] * (IDX_HEADS ** -0.5 * IDX_DIM ** -0.5)
    iw_cols = [iw[:, IDX_DIM + h:IDX_DIM + h + 1] for h in range(IDX_HEADS)]
    col_iota = lax.broadcasted_iota(I32, (qb, kc), 1)

    def score_body(ci, carry):
        off = pl.multiple_of(ci * kc, kc)
        rel = jnp.maximum(_dot(iq_stack, ik3_ref[:, pl.ds(off, kc)]), 0.0)
        sc = rel[0:qb] * iw_cols[0]
        for h in range(1, IDX_HEADS):
            sc = sc + rel[h * qb:(h + 1) * qb] * iw_cols[h]
        sc = sc + 0.0
        sc = jnp.where(off + col_iota <= q_pos, sc, -jnp.inf)
        bits = lax.bitcast_convert_type(sc, I32)
        key = bits ^ ((bits >> 31) & 0x7FFFFFFF)
        keys_ref[:, pl.ds(off, kc)] = key
        key_t = key.T
        keyt_ref[pl.ds(off, kc), :] = key_t
        hi_ref[pl.ds(off, kc), :] = (key_t >> 16).astype(I16)
        return carry

    lax.fori_loop(0, n_kc, score_body, 0)

    def count16(ref, pred):
        rows = 2 * SUBLANES
        def body(ci, acc):
            off = pl.multiple_of(ci * kc, kc)
            x = jnp.where(pred(ref[pl.ds(off, kc), :], off), jnp.int16(1), jnp.int16(0))
            parts = [x[r * rows:(r + 1) * rows] for r in range(kc // rows)]
            while len(parts) > 1:
                parts = [parts[i] + parts[i + 1] for i in range(0, len(parts), 2)]
            return acc + parts[0]
        acc = lax.fori_loop(0, n_kc, body, jnp.zeros((rows, qb), I16))
        return jnp.sum(acc.astype(F32), axis=0, keepdims=True)

    def search16(ref, want, known_u=0, n_bits=16):
        def bit_body(i, t_u):
            cand_u = t_u | lax.shift_left(jnp.int32(1), jnp.int32(n_bits - 1) - i)
            cand = (cand_u - 32768).astype(I16)
            cnt = count16(ref, lambda kk_, off: kk_ >= cand)
            return jnp.where(cnt >= want, cand_u, t_u)
        return lax.fori_loop(0, n_bits, bit_body, jnp.full((1, qb), known_u, I32)) - 32768

    t_hi = search16(hi_ref, float(topk))
    t_hi16 = t_hi.astype(I16)
    want_lo = float(topk) - count16(hi_ref, lambda kk_, off: kk_ > t_hi16)

    def lo_body(ci, carry):
        off = pl.multiple_of(ci * kc, kc)
        low = ((keyt_ref[pl.ds(off, kc), :] & 0xFFFF) - 32768).astype(I16)
        lo_ref[pl.ds(off, kc), :] = jnp.where(hi_ref[pl.ds(off, kc), :] == t_hi16, low, jnp.int16(-32768))
        return carry

    lax.fori_loop(0, n_kc, lo_body, 0)
    t_lo = search16(lo_ref, want_lo)
    tau_row = lax.shift_left(t_hi, 16) | (t_lo + 32768)
    t_lo16 = t_lo.astype(I16)
    need = want_lo - count16(lo_ref, lambda kk_, off: kk_ > t_lo16)

    idx_bits = max(1, int(math.ceil(math.log2(seq))))
    row_iota = lax.broadcasted_iota(I32, (kc, qb), 0)

    def tie_body(ci, carry):
        off = pl.multiple_of(ci * kc, kc)
        rev = (seq - 1) - (off + row_iota)
        lo_ref[pl.ds(off, kc), :] = jnp.where(keyt_ref[pl.ds(off, kc), :] == tau_row, rev, -1).astype(I16)
        return carry

    lax.fori_loop(0, n_kc, tie_body, 0)
    c_star_row = (seq - 1) - search16(lo_ref, need, known_u=32768, n_bits=idx_bits)

    def to_column(row):
        as_f32 = lax.bitcast_convert_type(jnp.broadcast_to(row, (LANES, qb)), F32)
        return lax.bitcast_convert_type(as_f32.T, I32)[:, 0:1]

    tau = to_column(tau_row)
    c_star = to_column(c_star_row)

    q = q_ref[...]
    cos_q = jnp.concatenate([cos64_ref[pl.ds(q0, qb), :]] * 2, axis=1)
    sin_q = jnp.concatenate([sin64_ref[pl.ds(q0, qb), :]] * 2, axis=1)
    q = (q * cos_q + _rot_half(q, HEAD_DIM // 2) * sin_q) * (HEAD_DIM ** -0.5)
    q_lhs = []
    for h in range(N_HEADS):
        half = q[:, 128 * (h >> 1):128 * (h >> 1) + 128]
        half = jnp.where((lane >> 6) == (h & 1), half, 0.0)
        q_lhs.append((pltpu.roll(half, 64, 1) if h & 1 else half).astype(BF16))
    neg = -1e30
    q_stack = jnp.concatenate(q_lhs, axis=0)
    m_ref[...] = jnp.full_like(m_ref, neg)
    acc_ref[...] = jnp.zeros_like(acc_ref)

    def attn_body(ci, carry):
        off = pl.multiple_of(ci * kc, kc)
        kk_ = keys_ref[:, pl.ds(off, kc)]
        colp = off + col_iota
        sel = ((kk_ > tau) | ((kk_ == tau) & (colp <= c_star))) & (colp <= q_pos)
        bias = jnp.where(sel, 0.0, neg)
        kx = kx_ref[:, pl.ds(off, kc)]
        vx = vx_ref[pl.ds(off, kc), :]
        s_ref[...] = _dot(q_stack, kx) + jnp.concatenate([bias] * N_HEADS, axis=0)
        m_old = m_ref[...]
        m_new = jnp.maximum(m_old, jnp.max(s_ref[...], axis=-1, keepdims=True))
        for t in range(kc // LANES):
            sl = slice(t * LANES, (t + 1) * LANES)
            p_ref[:, sl] = jnp.exp(s_ref[:, sl] - m_new).astype(BF16)
        acc_ref[...] = jnp.exp(m_old - m_new) * acc_ref[...] + _dot(p_ref[...], vx)
        m_ref[...] = m_new
        return carry

    lax.fori_loop(0, n_kc, attn_body, 0)
    halves = []
    for pair in range(N_HEADS // 2):
        outs = []
        for h in (2 * pair, 2 * pair + 1):
            a = acc_ref[h * qb:(h + 1) * qb]
            outs.append(a / pltpu.roll(a, 64, 1))
        halves.append(jnp.where(lane < HEAD_DIM, outs[0], pltpu.roll(outs[1], 64, 1)))
    o_ref[...] = jnp.concatenate(halves, axis=1)


def _dsa(dsa_p, idx_k_norm, cos32, sin32, cos64, sin64):
    b, s, _ = dsa_p.shape
    qb = DSA_QBLOCK
    kc = min(DSA_KEY_CHUNK, s)
    topk = min(DSA_TOPK_MAX, s // 4)
    assert s % kc == 0 and s <= 2 ** 15, "key positions are searched as packed 16-bit values"
    gpad = jnp.pad(idx_k_norm.astype(F32), (0, 128 - IDX_DIM)).reshape(1, 128)
    full = lambda a: pl.BlockSpec(a.shape, lambda i, j: (0,) * a.ndim)
    return pl.pallas_call(
        functools.partial(_dsa_kernel, seq=s, topk=topk, kc=kc),
        grid=(b, s // qb),
        in_specs=[pl.BlockSpec((None, qb, 256), lambda i, j: (i, j, 0)),
                  pl.BlockSpec((None, qb, 128), lambda i, j: (i, j, 2)),
                  pl.BlockSpec((None, s, 128), lambda i, j: (i, 0, 3)),
                  pl.BlockSpec((None, s, 128), lambda i, j: (i, 0, 4)),
                  full(cos32), full(sin32), full(cos64), full(sin64), full(gpad)],
        out_specs=pl.BlockSpec((None, qb, 256), lambda i, j: (i, j, 0)),
        out_shape=jax.ShapeDtypeStruct((b, s, 256), F32),
        scratch_shapes=[pltpu.VMEM((128, s), BF16), pltpu.VMEM((128, s), BF16), pltpu.VMEM((s, 128), BF16),
                        pltpu.VMEM((qb, s), I32), pltpu.VMEM((s, qb), I32),
                        pltpu.VMEM((s, qb), I16), pltpu.VMEM((s, qb), I16),
                        pltpu.VMEM((N_HEADS * qb, LANES), F32), pltpu.VMEM((N_HEADS * qb, LANES), F32),
                        pltpu.VMEM((N_HEADS * qb, kc), F32), pltpu.VMEM((N_HEADS * qb, kc), BF16)],
        compiler_params=_cparams("parallel", "arbitrary"), name="dsa",
    )(dsa_p, dsa_p, dsa_p, dsa_p, cos32, sin32, cos64, sin64, gpad)


def _mix_cross_kernel(h_ref, o0_ref, o1_ref, o2_ref, o3_ref, wout_ref, g_ref, wq_ref, k_ref, v_ref, wo_ref,
                      out_ref):
    h = h_ref[...]
    for i, o_ref in enumerate((o0_ref, o1_ref, o2_ref, o3_ref)):
        h = h + _dot(o_ref[...].astype(BF16), wout_ref[i * GROUP_WIDTH:(i + 1) * GROUP_WIDTH, :])
    q = _dot(_rms(h, g_ref[...]).astype(BF16), wq_ref[...]) * (X_HEAD_DIM ** -0.5)
    outs = []
    for hd in range(X_HEADS):
        sl = slice(hd * X_HEAD_DIM, (hd + 1) * X_HEAD_DIM)
        s = _dot_nt(q[:, sl].astype(BF16), k_ref[:, sl])
        p = jnp.exp(s - jnp.max(s, axis=-1, keepdims=True))
        p = p / jnp.sum(p, axis=-1, keepdims=True)
        outs.append(_dot(p.astype(BF16), v_ref[:, sl]))
    o = jnp.concatenate(outs, axis=1).astype(BF16)
    out_ref[...] = h + _dot(o, wo_ref[...])


def _mix_cross(h2d, outs, w_out, g_cross, wq, k_mem, v_mem, wo, seq, tm):
    t, d = h2d.shape
    n_mem = k_mem.shape[1]
    per_seq = seq // tm
    tok = lambda w: pl.BlockSpec((tm, w), lambda i: (i, 0))
    const = lambda a: pl.BlockSpec(a.shape, lambda i: (0,) * a.ndim)
    mem = pl.BlockSpec((None, n_mem, d), lambda i: (i // per_seq, 0, 0))
    return pl.pallas_call(
        _mix_cross_kernel,
        grid=(t // tm,),
        in_specs=[tok(d)] + [tok(GROUP_WIDTH)] * 4 + [const(w_out), pl.BlockSpec((1, d), lambda i: (0, 0)),
                                                     const(wq), mem, mem, const(wo)],
        out_specs=tok(d),
        out_shape=jax.ShapeDtypeStruct((t, d), F32),
        compiler_params=_cparams("parallel"), name="mix_cross",
    )(h2d, *outs, w_out, g_cross.reshape(1, d), wq, k_mem, v_mem, wo)


def _ffn_kernel(h_ref, g_ref, wg_ref, wv_ref, cw_ref, cb_ref, wd_ref, gf_ref, out_ref, tail_ref, acc_ref,
                *, per_seq, n_chunks, final_norm):
    tm = h_ref.shape[0]

    @pl.when(pl.program_id(0) % per_seq == 0)
    def _():
        tail_ref[...] = jnp.zeros_like(tail_ref)

    h = h_ref[...]
    xb = _rms(h, g_ref[...]).astype(BF16)
    acc_ref[...] = jnp.zeros_like(acc_ref)

    def body(ci, carry):
        gate = _dot(xb, wg_ref[ci])
        val = _dot(xb, wv_ref[ci])
        tail = tail_ref[ci]
        cw = cw_ref[ci]
        conv = gate * cw[FFN_CONV - 1:FFN_CONV, :] + cb_ref[ci]
        for m in range(1, FFN_CONV):
            conv = conv + _shift_rows(tail, gate, m) * cw[FFN_CONV - 1 - m:FFN_CONV - m, :]
        tail_ref[ci] = gate[tm - SUBLANES:tm, :]
        acc_ref[...] += _dot((_silu(conv) * val).astype(BF16), wd_ref[ci])
        return carry

    lax.fori_loop(0, n_chunks, body, 0, unroll=True)
    out = h + acc_ref[...]
    if final_norm:
        out = _rms(out, gf_ref[...])
    out_ref[...] = out


def _ffn(h2d, g_ffn, w_up, conv_w, conv_b, w_down, g_final, seq, tm, final_norm):
    t, d = h2d.shape
    fc = FF_CHUNK
    nck = D_FF // fc
    chunked = lambda w: w.reshape(w.shape[0], nck, fc).transpose(1, 0, 2)
    wg = chunked(w_up[:, :D_FF]).astype(BF16)
    wv = chunked(w_up[:, D_FF:]).astype(BF16)
    cw = chunked(conv_w.astype(F32))
    cb = conv_b.astype(F32).reshape(nck, 1, fc)
    wd = w_down.reshape(nck, fc, d).astype(BF16)
    const = lambda a: pl.BlockSpec(a.shape, lambda i: (0,) * a.ndim)
    row = pl.BlockSpec((1, d), lambda i: (0, 0))
    return pl.pallas_call(
        functools.partial(_ffn_kernel, per_seq=seq // tm, n_chunks=nck, final_norm=final_norm),
        grid=(t // tm,),
        in_specs=[pl.BlockSpec((tm, d), lambda i: (i, 0)), row, const(wg), const(wv), const(cw), const(cb),
                  const(wd), row],
        out_specs=pl.BlockSpec((tm, d), lambda i: (i, 0)),
        out_shape=jax.ShapeDtypeStruct((t, d), F32),
        scratch_shapes=[pltpu.VMEM((nck, SUBLANES, fc), F32), pltpu.VMEM((tm, d), F32)],
        compiler_params=_cparams("arbitrary"), name="conv_glu",
    )(h2d, g_ffn.reshape(1, d), wg, wv, cw, cb, wd, g_final.reshape(1, d))


def _rope_tables(seq, head_dim, width):
    half = head_dim // 2
    inv = ROPE_THETA ** (-jnp.arange(half, dtype=F32) / half)
    ang = jnp.arange(seq, dtype=F32)[:, None] * inv[None, :]
    reps = width // half
    return jnp.tile(jnp.cos(ang), (1, reps)), jnp.tile(jnp.sin(ang), (1, reps))


def _split_w_in(w):
    ret = w[:, 0:768]
    rw = w[:, 768:1792]
    ssm = jnp.pad(w[:, 1792:2820], ((0, 0), (0, 1152 - 1028)))
    d = w[:, 2820:3368]
    dsa = jnp.concatenate([d[:, 0:256], d[:, 384:512], d[:, 256:384], d[:, 512:548],
                           jnp.zeros((w.shape[0], 128 - 36), w.dtype)], axis=1)
    return [t.astype(BF16) for t in (ret, rw, ssm, dsa)]


def kernel(x, mem, norm_mix, w_in, rwkv_mu, rwkv_w0, rwkv_w2, rwkv_a0, rwkv_a2, rwkv_g2, rwkv_k_k, rwkv_k_a, rwkv_r_k, rwkv_ln_w, rwkv_ln_b, ssm_conv_w, ssm_conv_b, ssm_dt_bias, ssm_a_log, ssm_d, ssm_norm, idx_k_norm, w_out, norm_cross, norm_mem, wq_x, wk_x, wv_x, wo_x, norm_ffn, w_up, ffn_conv_w, ffn_conv_b, w_down, norm_final):
    b, s, d = x.shape
    n_mem = mem.shape[1]
    depth = w_in.shape[0]
    t = b * s
    tm = min(512, s)
    cos32, sin32 = _rope_tables(s, RET_QK_DIM, 128)
    cos64, sin64 = _rope_tables(s, HEAD_DIM, 128)
    mem2d = mem.reshape(b * n_mem, d)
    h = x.reshape(t, d)
    for l in range(depth):
        ret_p, rw_p, ssm_p, dsa_p = _norm_matmul(h, norm_mix[l], _split_w_in(w_in[l]), [F32] * 4, tm)
        o_ret = _retention(ret_p.reshape(b, s, -1), cos32, sin32)
        o_rw = _rwkv(rw_p.reshape(b, s, -1), rwkv_mu[l], rwkv_w0[l], rwkv_w2[l], rwkv_a0[l], rwkv_a2[l],
                     rwkv_g2[l], rwkv_k_k[l], rwkv_k_a[l], rwkv_r_k[l], rwkv_ln_w[l], rwkv_ln_b[l])
        o_ssm = _ssd(ssm_p.reshape(b, s, -1), ssm_conv_w[l], ssm_conv_b[l], ssm_dt_bias[l], ssm_a_log[l],
                     ssm_d[l], ssm_norm[l])
        o_dsa = _dsa(dsa_p.reshape(b, s, -1), idx_k_norm[l], cos32, sin32, cos64, sin64)
        k_mem, v_mem = _norm_matmul(mem2d, norm_mem[l], [wk_x[l].astype(BF16), wv_x[l].astype(BF16)],
                                    [BF16, BF16], min(512, b * n_mem))
        outs = [o.reshape(t, GROUP_WIDTH) for o in (o_ret, o_rw, o_ssm, o_dsa)]
        h = _mix_cross(h, outs, w_out[l].astype(BF16), norm_cross[l], wq_x[l].astype(BF16),
                       k_mem.reshape(b, n_mem, d), v_mem.reshape(b, n_mem, d), wo_x[l].astype(BF16), s, tm)
        h = _ffn(h, norm_ffn[l], w_up[l], ffn_conv_w[l], ffn_conv_b[l], w_down[l], norm_final, s, tm,
                 final_norm=(l == depth - 1))
    return h.reshape(b, s, d)
```
